```python
import jax, jax.numpy as jnp
from jax import lax
import numpy as np

D_MODEL = 1024
BATCH = 8
SEQ = 2048
DEPTH = 4

N_MEM = 256
MLP_HIDDEN = 4 * D_MODEL
RMS_EPS = 1e-6
LN_EPS = 1e-5
MASK_VALUE = -1e30
MIN_FORGET = 1e-20

N_MIXERS = 3
LAYER_KINDS = tuple(i % N_MIXERS for i in range(DEPTH))
N_PER_KIND = tuple(LAYER_KINDS.count(k) for k in range(N_MIXERS))

HGRN_HEAD_DIM = 128
HGRN_HEADS = D_MODEL // HGRN_HEAD_DIM
HGRN_WIDTH = HGRN_HEADS * HGRN_HEAD_DIM
HGRN_CHUNK = 64

SWA_HEAD_DIM = 64
SWA_Q_HEADS = D_MODEL // SWA_HEAD_DIM
SWA_KV_HEADS = 2
SWA_GROUP = SWA_Q_HEADS // SWA_KV_HEADS
SWA_WINDOW = 128
SWA_BLOCK = SWA_WINDOW
SWA_Q_WIDTH = SWA_Q_HEADS * SWA_HEAD_DIM
SWA_KV_WIDTH = SWA_KV_HEADS * SWA_HEAD_DIM
ROPE_THETA = 10000.0

CONV_DIM = D_MODEL
CONV_WIDTH = 31

XA_HEADS = 4
XA_HEAD_DIM = 128
XA_WIDTH = XA_HEADS * XA_HEAD_DIM

A_IN = 4 * HGRN_WIDTH + XA_WIDTH
B_IN = SWA_Q_WIDTH + 2 * SWA_KV_WIDTH + XA_WIDTH
C_IN = 2 * CONV_DIM + XA_WIDTH

kernel_name = "hybrid_hgrn2_swa_conformer_trunk"


def rms_norm(x, g):
    xf = x.astype(jnp.float32)
    y = xf * lax.rsqrt(jnp.mean(xf * xf, axis=-1, keepdims=True) + RMS_EPS)
    return (y * g.astype(jnp.float32)).astype(x.dtype)


def layer_norm(x, g, b):
    xf = x.astype(jnp.float32)
    mu = jnp.mean(xf, axis=-1, keepdims=True)
    var = jnp.mean(jnp.square(xf - mu), axis=-1, keepdims=True)
    y = (xf - mu) * lax.rsqrt(var + LN_EPS) * g.astype(jnp.float32) + b.astype(jnp.float32)
    return y.astype(x.dtype)


def rope_tables(positions):
    half = SWA_HEAD_DIM // 2
    inv_freq = ROPE_THETA ** (-jnp.arange(half, dtype=jnp.float32) / half)
    ang = positions.astype(jnp.float32)[..., None] * inv_freq
    return jnp.cos(ang)[:, :, None, :], jnp.sin(ang)[:, :, None, :]


def apply_rope(x, cos, sin):
    half = x.shape[-1] // 2
    xf = x.astype(jnp.float32)
    x1, x2 = xf[..., :half], xf[..., half:]
    return jnp.concatenate([x1 * cos - x2 * sin, x2 * cos + x1 * sin], axis=-1).astype(x.dtype)


def hgrn_lower_bounds(logits):
    p = jax.nn.softmax(logits.astype(jnp.float32), axis=0)
    return jnp.cumsum(p, axis=0) - p[0:1]


def hgrn2_chunked(q, k, v, log_f):
    B, H, T, dk = q.shape
    dv = v.shape[-1]
    C = HGRN_CHUNK
    N = T // C

    def to_chunks(a):
        return a.reshape(B, H, N, C, a.shape[-1]).transpose(2, 0, 1, 3, 4)

    causal = jnp.tril(jnp.ones((C, C), dtype=bool))[:, :, None]

    def step(S, inp):
        qc, kc, vc, gc = inp
        b = jnp.cumsum(gc, axis=2)
        diff = b[:, :, :, None, :] - b[:, :, None, :, :]
        decay = jnp.where(causal, jnp.exp(jnp.minimum(diff, 0.0)), 0.0)
        attn = jnp.einsum('bhik,bhijk,bhjk->bhij', qc, decay, kc)
        o = (jnp.einsum('bhij,bhjv->bhiv', attn, vc)
             + jnp.einsum('bhik,bhkv->bhiv', qc * jnp.exp(b), S))
        b_last = b[:, :, -1:, :]
        S = (jnp.exp(b_last[:, :, 0, :])[..., None] * S
             + jnp.einsum('bhjk,bhjv->bhkv', kc * jnp.exp(b_last - b), vc))
        return S, o

    S0 = jnp.zeros((B, H, dk, dv), jnp.float32)
    _, o = lax.scan(step, S0, (to_chunks(q), to_chunks(k), to_chunks(v), to_chunks(log_f)))
    return o.transpose(1, 2, 0, 3, 4).reshape(B, H, T, dv)


def hgrn2_mixer(proj, lb, o_norm_g):
    B, T, _ = proj.shape
    q, f_raw, inp, gate = jnp.split(proj, 4, axis=-1)

    def heads(a):
        return a.reshape(B, T, HGRN_HEADS, HGRN_HEAD_DIM).transpose(0, 2, 1, 3).astype(jnp.float32)

    lb_h = lb.astype(jnp.float32).reshape(1, HGRN_HEADS, 1, HGRN_HEAD_DIM)
    fr = heads(f_raw)
    f = lb_h + (1.0 - lb_h) * jax.nn.sigmoid(fr)
    log_f = jnp.log(jnp.maximum(f, MIN_FORGET))
    k = 1.0 - f
    o = hgrn2_chunked(heads(q), k, heads(inp), log_f)
    o = o.transpose(0, 2, 1, 3)
    o = rms_norm(o, o_norm_g.reshape(HGRN_HEADS, HGRN_HEAD_DIM))
    return o.reshape(B, T, HGRN_WIDTH).astype(proj.dtype) * jax.nn.silu(gate)


def swa_mixer(proj, cos, sin, q_g, k_g, sinks):
    B, T, _ = proj.shape
    hd, KV, G, BLK = SWA_HEAD_DIM, SWA_KV_HEADS, SWA_GROUP, SWA_BLOCK
    q = proj[..., :SWA_Q_WIDTH].reshape(B, T, SWA_Q_HEADS, hd)
    k = proj[..., SWA_Q_WIDTH:SWA_Q_WIDTH + SWA_KV_WIDTH].reshape(B, T, KV, hd)
    v = proj[..., SWA_Q_WIDTH + SWA_KV_WIDTH:].reshape(B, T, KV, hd)
    q = apply_rope(rms_norm(q, q_g), cos, sin)
    k = apply_rope(rms_norm(k, k_g), cos, sin)
    N = T // BLK
    qb = q.reshape(B, N, BLK, KV, G, hd)

    def with_prev_block(a):
        ab = jnp.pad(a, ((0, 0), (BLK, 0), (0, 0), (0, 0))).reshape(B, N + 1, BLK, KV, hd)
        return jnp.concatenate([ab[:, :-1], ab[:, 1:]], axis=2)

    kb, vb = with_prev_block(k), with_prev_block(v)
    scores = jnp.einsum('bnqkgd,bnskd->bnkgqs', qb, kb).astype(jnp.float32) * (hd ** -0.5)
    qi = jnp.arange(BLK)[:, None]
    si = jnp.arange(2 * BLK)[None, :]
    rel = qi + BLK - si
    in_window = (rel >= 0) & (rel < SWA_WINDOW)
    key_pos = jnp.arange(N)[:, None] * BLK - BLK + si
    valid = in_window[None] & (key_pos >= 0)[:, None, :]
    scores = jnp.where(valid[None, :, None, None], scores, MASK_VALUE)
    sink = sinks.astype(jnp.float32).reshape(KV, G)[None, None, :, :, None, None]
    m = jnp.maximum(jnp.max(scores, axis=-1, keepdims=True), sink)
    p = jnp.exp(scores - m)
    p = p / (jnp.sum(p, axis=-1, keepdims=True) + jnp.exp(sink - m))
    out = jnp.einsum('bnkgqs,bnskd->bnqkgd', p.astype(v.dtype), vb)
    return out.reshape(B, T, SWA_Q_WIDTH)


def conformer_conv_mixer(proj, conv_w, conv_b, ln_g, ln_b):
    a, gate = proj[..., :CONV_DIM], proj[..., CONV_DIM:]
    u = a * jax.nn.sigmoid(gate)
    y = lax.conv_general_dilated(
        u, conv_w[:, None, :].astype(u.dtype), window_strides=(1,),
        padding=[(CONV_WIDTH - 1, 0)], dimension_numbers=('NWC', 'WIO', 'NWC'),
        feature_group_count=CONV_DIM) + conv_b
    y = layer_norm(y, ln_g, ln_b)
    return jax.nn.silu(y)


def memory_cross_attention(xq, memn, w_kv, q_g, k_g):
    B, T, _ = xq.shape
    M = memn.shape[1]
    q = rms_norm(xq.reshape(B, T, XA_HEADS, XA_HEAD_DIM), q_g)
    kv = memn @ w_kv
    k = rms_norm(kv[..., :XA_WIDTH].reshape(B, M, XA_HEADS, XA_HEAD_DIM), k_g)
    v = kv[..., XA_WIDTH:].reshape(B, M, XA_HEADS, XA_HEAD_DIM)
    s = jnp.einsum('bthd,bmhd->bhtm', q, k).astype(jnp.float32) * (XA_HEAD_DIM ** -0.5)
    p = jax.nn.softmax(s, axis=-1).astype(v.dtype)
    return jnp.einsum('bhtm,bmhd->bthd', p, v).reshape(B, T, XA_WIDTH)


def squared_relu_mlp(h, w_up, w_down):
    return jnp.square(jax.nn.relu(h @ w_up)) @ w_down


def setup_inputs(seed: int = 0) -> dict:
    key = jax.random.key(seed)
    ks = iter(jax.random.split(key, 64))

    def nrm(shape, scale):
        return scale * jax.random.normal(next(ks), shape, jnp.float32)

    def gain(shape):
        return 1.0 + 0.05 * jax.random.normal(next(ks), shape, jnp.float32)

    NA, NB, NC = N_PER_KIND
    res = 0.5
    offsets = jax.random.randint(next(ks), (BATCH, 1), 0, 4096, dtype=jnp.int32)
    positions = offsets + jnp.arange(SEQ, dtype=jnp.int32)[None, :]
    return {
        "x": nrm((BATCH, SEQ, D_MODEL), 1.0),
        "mem": nrm((BATCH, N_MEM, D_MODEL), 1.0),
        "positions": positions,
        "norm_mix_g": gain((DEPTH, D_MODEL)),
        "norm_mlp_g": gain((DEPTH, D_MODEL)),
        "mem_norm_g": gain((DEPTH, D_MODEL)),
        "xa_w_kv": nrm((DEPTH, D_MODEL, 2 * XA_WIDTH), D_MODEL ** -0.5),
        "xa_q_norm_g": gain((DEPTH, XA_HEAD_DIM)),
        "xa_k_norm_g": gain((DEPTH, XA_HEAD_DIM)),
        "mlp_w_up": nrm((DEPTH, D_MODEL, MLP_HIDDEN), D_MODEL ** -0.5),
        "mlp_w_down": nrm((DEPTH, MLP_HIDDEN, D_MODEL), res * MLP_HIDDEN ** -0.5),
        "hgrn_lb_logits": nrm((DEPTH, HGRN_WIDTH), 0.5),
        "a_w_in": nrm((NA, D_MODEL, A_IN), D_MODEL ** -0.5),
        "a_o_norm_g": gain((NA, HGRN_WIDTH)),
        "a_w_out": nrm((NA, HGRN_WIDTH + XA_WIDTH, D_MODEL), res * (HGRN_WIDTH + XA_WIDTH) ** -0.5),
        "b_w_in": nrm((NB, D_MODEL, B_IN), D_MODEL ** -0.5),
        "b_q_norm_g": gain((NB, SWA_HEAD_DIM)),
        "b_k_norm_g": gain((NB, SWA_HEAD_DIM)),
        "b_sinks": nrm((NB, SWA_Q_HEADS), 0.5),
        "b_w_out": nrm((NB, SWA_Q_WIDTH + XA_WIDTH, D_MODEL), res * (SWA_Q_WIDTH + XA_WIDTH) ** -0.5),
        "c_w_in": nrm((NC, D_MODEL, C_IN), D_MODEL ** -0.5),
        "c_conv_w": nrm((NC, CONV_WIDTH, CONV_DIM), CONV_WIDTH ** -0.5),
        "c_conv_b": nrm((NC, CONV_DIM), 0.02),
        "c_ln_g": gain((NC, CONV_DIM)),
        "c_ln_b": nrm((NC, CONV_DIM), 0.02),
        "c_w_out": nrm((NC, CONV_DIM + XA_WIDTH, D_MODEL), res * (CONV_DIM + XA_WIDTH) ** -0.5),
    }


def reference(x, mem, positions, norm_mix_g, norm_mlp_g, mem_norm_g, xa_w_kv, xa_q_norm_g,
              xa_k_norm_g, mlp_w_up, mlp_w_down, hgrn_lb_logits, a_w_in, a_o_norm_g, a_w_out,
              b_w_in, b_q_norm_g, b_k_norm_g, b_sinks, b_w_out, c_w_in, c_conv_w, c_conv_b,
              c_ln_g, c_ln_b, c_w_out):
    lb_all = hgrn_lower_bounds(hgrn_lb_logits)
    cos, sin = rope_tables(positions)
    for layer in range(DEPTH):
        kind = LAYER_KINDS[layer]
        idx = LAYER_KINDS[:layer].count(kind)
        h = rms_norm(x, norm_mix_g[layer])
        memn = rms_norm(mem, mem_norm_g[layer])
        if kind == 0:
            proj = h @ a_w_in[idx]
            mix = hgrn2_mixer(proj[..., :-XA_WIDTH], lb_all[layer], a_o_norm_g[idx])
            w_out = a_w_out[idx]
        elif kind == 1:
            proj = h @ b_w_in[idx]
            mix = swa_mixer(proj[..., :-XA_WIDTH], cos, sin, b_q_norm_g[idx], b_k_norm_g[idx], b_sinks[idx])
            w_out = b_w_out[idx]
        else:
            proj = h @ c_w_in[idx]
            mix = conformer_conv_mixer(proj[..., :-XA_WIDTH], c_conv_w[idx], c_conv_b[idx], c_ln_g[idx], c_ln_b[idx])
            w_out = c_w_out[idx]
        xa = memory_cross_attention(proj[..., -XA_WIDTH:], memn, xa_w_kv[layer],
                                    xa_q_norm_g[layer], xa_k_norm_g[layer])
        x = x + jnp.concatenate([mix, xa], axis=-1) @ w_out
        h2 = rms_norm(x, norm_mlp_g[layer])
        x = x + squared_relu_mlp(h2, mlp_w_up[layer], mlp_w_down[layer])
    return x
```

```python
import functools

import numpy as np
import jax
import jax.numpy as jnp
from jax import lax
from jax.experimental import pallas as pl
from jax.experimental.pallas import tpu as pltpu

F32 = jnp.float32
BF16 = jnp.bfloat16

RMS_EPS = 1e-6
LN_EPS = 1e-5
MASK_VALUE = -1e30
MIN_FORGET = 1e-20
ROPE_THETA = 10000.0

LANES = 128
SUBLANES = 8

N_MEM = 256
XA_HEADS = 4
XA_HEAD_DIM = 128
XA_WIDTH = XA_HEADS * XA_HEAD_DIM
HGRN_HEADS = 8
HGRN_HEAD_DIM = 128
HGRN_TILE = 128
HGRN_SUB = SUBLANES
SWA_HEAD_DIM = 64
SWA_Q_HEADS = 16
SWA_KV_HEADS = 2
SWA_WINDOW = 128
CONV_WIDTH = 31
CONV_HALO = 32
CONV_TILE = 128
CONV_ROWS = 64

VMEM_LIMIT = 56 * 1024 * 1024


def _cparams(*sem):
    return pltpu.CompilerParams(dimension_semantics=sem, vmem_limit_bytes=VMEM_LIMIT)


def _dot(a, b):
    return jnp.dot(a, b, preferred_element_type=F32)


def _dot_nt(a, b):
    return lax.dot_general(a, b, (((1,), (1,)), ((), ())), preferred_element_type=F32)


def _dot_tn(a, b):
    return lax.dot_general(a, b, (((0,), (0,)), ((), ())), preferred_element_type=F32)


def _rms(x, g):
    return x * lax.rsqrt(jnp.mean(x * x, axis=-1, keepdims=True) + RMS_EPS) * g


def _sigmoid(x):
    return 1.0 / (1.0 + jnp.exp(-x))


def _in_proj_kernel(x_ref, g_ref, w_ref, mix_ref, qx_ref, *, n_mix, tn, head_major):
    h = _rms(x_ref[...], g_ref[...]).astype(BF16)
    for c0 in range(0, n_mix, tn):
        r = _dot(h, w_ref[:, c0:c0 + tn]).astype(BF16)
        if head_major:
            for j in range(tn // LANES):
                mix_ref[c0 // LANES + j] = r[:, j * LANES:(j + 1) * LANES]
        else:
            mix_ref[:, c0:c0 + tn] = r
    qx_ref[...] = _dot(h, w_ref[:, n_mix:]).astype(BF16)


def _in_proj(x, g, w, *, head_major, tm=512):
    m, d = x.shape
    n = w.shape[1]
    n_mix = n - XA_WIDTH
    tn = 512 if n_mix % 512 == 0 else 256
    assert m % tm == 0 and n_mix % tn == 0
    if head_major:
        mix_shape = jax.ShapeDtypeStruct((n_mix // LANES, m, LANES), BF16)
        mix_spec = pl.BlockSpec((n_mix // LANES, tm, LANES), lambda i: (0, i, 0))
    else:
        mix_shape = jax.ShapeDtypeStruct((m, n_mix), BF16)
        mix_spec = pl.BlockSpec((tm, n_mix), lambda i: (i, 0))
    return pl.pallas_call(
        functools.partial(_in_proj_kernel, n_mix=n_mix, tn=tn, head_major=head_major),
        grid=(m // tm,),
        in_specs=[pl.BlockSpec((tm, d), lambda i: (i, 0)),
                  pl.BlockSpec((1, d), lambda i: (0, 0)),
                  pl.BlockSpec((d, n), lambda i: (0, 0))],
        out_specs=[mix_spec, pl.BlockSpec((tm, XA_WIDTH), lambda i: (i, 0))],
        out_shape=[mix_shape, jax.ShapeDtypeStruct((m, XA_WIDTH), BF16)],
        compiler_params=_cparams("parallel"),
        name="in_proj",
    )(x, g.reshape(1, d), w)


def _mem_kv_kernel(mem_ref, g_ref, w_ref, kg_ref, k_ref, v_ref):
    memn = _rms(mem_ref[...], g_ref[...]).astype(BF16)
    kv = _dot(memn, w_ref[...])
    kg = kg_ref[...]
    for h in range(XA_HEADS):
        sl = slice(h * XA_HEAD_DIM, (h + 1) * XA_HEAD_DIM)
        k_ref[:, sl] = _rms(kv[:, sl], kg).astype(BF16)
    v_ref[...] = kv[:, XA_WIDTH:].astype(BF16)


def _mem_kv(mem, mem_norm_g, w_kv, k_g):
    b, nm, d = mem.shape
    depth = w_kv.shape[0]
    kv_shape = jax.ShapeDtypeStruct((depth, b, nm, XA_WIDTH), BF16)
    kv_spec = pl.BlockSpec((None, None, nm, XA_WIDTH), lambda l, i: (l, i, 0, 0))
    return pl.pallas_call(
        _mem_kv_kernel,
        grid=(depth, b),
        in_specs=[pl.BlockSpec((None, nm, d), lambda l, i: (i, 0, 0)),
                  pl.BlockSpec((None, 1, d), lambda l, i: (l, 0, 0)),
                  pl.BlockSpec((None, d, 2 * XA_WIDTH), lambda l, i: (l, 0, 0)),
                  pl.BlockSpec((None, 1, XA_HEAD_DIM), lambda l, i: (l, 0, 0))],
        out_specs=[kv_spec, kv_spec],
        out_shape=[kv_shape, kv_shape],
        compiler_params=_cparams("parallel", "parallel"),
        name="mem_kv",
    )(mem, mem_norm_g.reshape(depth, 1, d), w_kv, k_g.reshape(depth, 1, XA_HEAD_DIM))


def _out_proj_kernel(x_ref, mix_ref, qx_ref, k_ref, v_ref, qg_ref, w_ref, o_ref, *, head_major):
    qg = qg_ref[...]
    scale = XA_HEAD_DIM ** -0.5
    xa = []
    for h in range(XA_HEADS):
        sl = slice(h * XA_HEAD_DIM, (h + 1) * XA_HEAD_DIM)
        qn = _rms(qx_ref[:, sl].astype(F32), qg).astype(BF16)
        s = _dot_nt(qn, k_ref[:, sl]) * scale
        e = jnp.exp(s - jnp.max(s, axis=-1, keepdims=True))
        den = jnp.sum(e, axis=-1, keepdims=True)
        xa.append((_dot(e.astype(BF16), v_ref[:, sl]) / den).astype(BF16))
    xa = jnp.concatenate(xa, axis=-1)
    if head_major:
        mix = jnp.concatenate([mix_ref[h] for h in range(mix_ref.shape[0])], axis=-1)
    else:
        mix = mix_ref[...]
    d_mix = mix.shape[-1]
    o_ref[...] = x_ref[...] + _dot(mix, w_ref[:d_mix, :]) + _dot(xa, w_ref[d_mix:, :])


def _out_proj(x, mix, qx, kx, vx, q_g, w_out, layer, *, batch, head_major, tm=512):
    m, d = x.shape
    t = m // batch
    nt = t // tm
    nm = kx.shape[2]
    if head_major:
        nh = mix.shape[0]
        mix_spec = pl.BlockSpec((nh, tm, LANES), lambda b, i: (0, b * nt + i, 0))
    else:
        mix_spec = pl.BlockSpec((tm, mix.shape[1]), lambda b, i: (b * nt + i, 0))
    kv_spec = pl.BlockSpec((None, None, nm, XA_WIDTH), lambda b, i: (layer, b, 0, 0))
    row_spec = lambda w: pl.BlockSpec((tm, w), lambda b, i: (b * nt + i, 0))
    return pl.pallas_call(
        functools.partial(_out_proj_kernel, head_major=head_major),
        grid=(batch, nt),
        in_specs=[row_spec(d), mix_spec, row_spec(XA_WIDTH), kv_spec, kv_spec,
                  pl.BlockSpec((1, XA_HEAD_DIM), lambda b, i: (0, 0)),
                  pl.BlockSpec(w_out.shape, lambda b, i: (0, 0))],
        out_specs=row_spec(d),
        out_shape=jax.ShapeDtypeStruct((m, d), F32),
        compiler_params=_cparams("parallel", "parallel"),
        name="out_proj",
    )(x, mix, qx, kx, vx, q_g.reshape(1, XA_HEAD_DIM), w_out)


def _mlp_kernel(x_ref, g_ref, wu_ref, wd_ref, o_ref, h_ref, acc_ref):
    k = pl.program_id(1)

    @pl.when(k == 0)
    def _():
        h_ref[...] = _rms(x_ref[...], g_ref[...]).astype(BF16)
        acc_ref[...] = x_ref[...]

    u = jnp.maximum(_dot(h_ref[...], wu_ref[...]), 0.0)
    acc_ref[...] += _dot((u * u).astype(BF16), wd_ref[...])

    @pl.when(k == pl.num_programs(1) - 1)
    def _():
        o_ref[...] = acc_ref[...]


def _mlp(x, g, w_up, w_down, *, tm=1024, th=1024):
    m, d = x.shape
    hid = w_up.shape[1]
    assert m % tm == 0 and hid % th == 0
    return pl.pallas_call(
        _mlp_kernel,
        grid=(m // tm, hid // th),
        in_specs=[pl.BlockSpec((tm, d), lambda i, k: (i, 0)),
                  pl.BlockSpec((1, d), lambda i, k: (0, 0)),
                  pl.BlockSpec((d, th), lambda i, k: (0, k)),
                  pl.BlockSpec((th, d), lambda i, k: (k, 0))],
        out_specs=pl.BlockSpec((tm, d), lambda i, k: (i, 0)),
        out_shape=jax.ShapeDtypeStruct((m, d), F32),
        scratch_shapes=[pltpu.VMEM((tm, d), BF16), pltpu.VMEM((tm, d), F32)],
        compiler_params=_cparams("parallel", "arbitrary"),
        name="mlp",
    )(x, g.reshape(1, d), w_up, w_down)


def _hgrn_levels(tile):
    levels = []
    s = HGRN_SUB
    while s < tile:
        levels.append(s)
        s *= 2
    return tuple(levels)


def _hgrn_exponent_matrix(tile):
    levels = _hgrn_levels(tile)
    w = np.zeros(((len(levels) + 1) * tile, tile), np.float32)
    for l, s in enumerate(levels):
        for i in range(tile):
            mid = (i // (2 * s)) * 2 * s + s - 1
            if (i // s) % 2 == 1:
                w[l * tile + i, mid + 1:i + 1] = 1.0
            else:
                w[l * tile + i, i + 1:mid + 1] = 1.0
    for i in range(tile):
        w[len(levels) * tile + i, :i + 1] = 1.0
    return w


def _hgrn_kernel(q_ref, f_ref, i_ref, g_ref, lg_ref, og_ref, w_ref, o_ref, st_ref, *, layer, tile):
    levels = _hgrn_levels(tile)
    n_lev = len(levels)

    @pl.when(pl.program_id(1) == 0)
    def _():
        st_ref[...] = jnp.zeros_like(st_ref)

    row = lax.broadcasted_iota(jnp.int32, (tile, LANES), 0)
    sub_row = row % HGRN_SUB
    ri = lax.broadcasted_iota(jnp.int32, (tile, tile), 0)
    ci = lax.broadcasted_iota(jnp.int32, (tile, tile), 1)
    wmat = w_ref[...]

    def head(h, carry):
        q = q_ref[h].astype(F32)
        v = i_ref[h].astype(F32)
        gate = g_ref[h].astype(F32)
        lg = lg_ref[h]
        e = jnp.exp(lg - jnp.max(lg, axis=0, keepdims=True))
        lb_num = jnp.zeros((1, LANES), F32)
        for r in range(1, layer + 1):
            lb_num = lb_num + e[r:r + 1, :]
        lb = lb_num / jnp.sum(e, axis=0, keepdims=True)
        f = lb + (1.0 - lb) * _sigmoid(f_ref[h].astype(F32))
        fc = jnp.maximum(f, MIN_FORGET)
        logf = jnp.log(fc)
        k = 1.0 - f

        l1 = logf.astype(BF16)
        r1 = logf - l1.astype(F32)
        l2 = r1.astype(BF16)
        l3 = (r1 - l2.astype(F32)).astype(BF16)
        ex = _dot(wmat, l1) + _dot(wmat, l2) + _dot(wmat, l3)
        b = ex[n_lev * tile:, :]
        b_last = b[tile - 1:tile, :]

        o = jnp.sum(q * k, axis=-1, keepdims=True) * v
        dec = None
        for d in range(1, HGRN_SUB):
            fs = fc if d == 1 else pltpu.roll(fc, d - 1, axis=0)
            dec = fs if d == 1 else dec * fs
            kd = pltpu.roll(k, d, axis=0)
            vd = pltpu.roll(v, d, axis=0)
            a = jnp.sum(jnp.where(sub_row >= d, q * kd * dec, 0.0), axis=-1, keepdims=True)
            o = o + a * vd

        attn = jnp.zeros((tile, tile), F32)
        for l, s in enumerate(levels):
            el = jnp.exp(jnp.minimum(ex[l * tile:(l + 1) * tile, :], 0.0))
            odd = (row // s) % 2 == 1
            qt = jnp.where(odd, q * el, 0.0).astype(BF16)
            kt = jnp.where(odd, 0.0, k * el).astype(BF16)
            pair = (ri // s == ci // s + 1) & ((ci // s) % 2 == 0)
            attn = jnp.where(pair, _dot_nt(qt, kt), attn)
        o = o + _dot(attn.astype(BF16), v.astype(BF16))

        st = st_ref[h]
        o = o + _dot_nt((q * jnp.exp(b)).astype(BF16), st.astype(BF16))
        kh = (k * jnp.exp(b_last - b)).astype(BF16)
        st_ref[h] = jnp.exp(b_last) * st + _dot_tn(v.astype(BF16), kh)

        o = _rms(o, og_ref[h])
        o_ref[h] = (o * (gate * _sigmoid(gate))).astype(BF16)
        return carry

    lax.fori_loop(0, HGRN_HEADS, head, 0)


def _hgrn(proj_hm, lb_logits, o_norm_g, layer, *, batch):
    _, m, _ = proj_hm.shape
    tile = HGRN_TILE
    nt = (m // batch) // tile
    depth = lb_logits.shape[0]
    wmat = jnp.asarray(_hgrn_exponent_matrix(tile), BF16)
    lg = lb_logits.reshape(depth, HGRN_HEADS, HGRN_HEAD_DIM).transpose(1, 0, 2)
    sec = lambda s: pl.BlockSpec((HGRN_HEADS, tile, LANES), lambda b, t: (s, b * nt + t, 0))
    return pl.pallas_call(
        functools.partial(_hgrn_kernel, layer=layer, tile=tile),
        grid=(batch, nt),
        in_specs=[sec(0), sec(1), sec(2), sec(3),
                  pl.BlockSpec((HGRN_HEADS, depth, LANES), lambda b, t: (0, 0, 0)),
                  pl.BlockSpec((HGRN_HEADS, 1, LANES), lambda b, t: (0, 0, 0)),
                  pl.BlockSpec(wmat.shape, lambda b, t: (0, 0))],
        out_specs=pl.BlockSpec((HGRN_HEADS, tile, LANES), lambda b, t: (0, b * nt + t, 0)),
        out_shape=jax.ShapeDtypeStruct((HGRN_HEADS, m, LANES), BF16),
        scratch_shapes=[pltpu.VMEM((HGRN_HEADS, HGRN_HEAD_DIM, HGRN_HEAD_DIM), F32)],
        compiler_params=_cparams("parallel", "arbitrary"),
        name="hgrn2",
    )(proj_hm, proj_hm, proj_hm, proj_hm, lg,
      o_norm_g.reshape(HGRN_HEADS, 1, HGRN_HEAD_DIM), wmat)


def _rope_table_kernel(pos_ref, cos_ref, sin_ref):
    lane = lax.broadcasted_iota(jnp.int32, (1, LANES), 1)
    half = SWA_HEAD_DIM // 2
    idx = (lane % half).astype(F32)
    inv_freq = jnp.exp(idx * (-np.log(ROPE_THETA) / half))
    ang = pos_ref[...].astype(F32) * inv_freq
    first = (lane % SWA_HEAD_DIM) < half
    cos_ref[...] = jnp.cos(ang)
    sin_ref[...] = jnp.where(first, -1.0, 1.0) * jnp.sin(ang)


def _rope_tables(positions, tm=1024):
    m = positions.size
    shape = jax.ShapeDtypeStruct((m, LANES), F32)
    spec = pl.BlockSpec((tm, LANES), lambda i: (i, 0))
    return pl.pallas_call(
        _rope_table_kernel,
        grid=(m // tm,),
        in_specs=[pl.BlockSpec((tm, 1), lambda i: (i, 0))],
        out_specs=[spec, spec],
        out_shape=[shape, shape],
        compiler_params=_cparams("parallel"),
        name="rope_tables",
    )(positions.reshape(m, 1))


def _swa_norm_rope(x, g, cos, sin, lo):
    sq = x * x
    ms_lo = jnp.sum(jnp.where(lo, sq, 0.0), axis=-1, keepdims=True) * (1.0 / SWA_HEAD_DIM)
    ms_hi = jnp.sum(jnp.where(lo, 0.0, sq), axis=-1, keepdims=True) * (1.0 / SWA_HEAD_DIM)
    y = x * jnp.where(lo, lax.rsqrt(ms_lo + RMS_EPS), lax.rsqrt(ms_hi + RMS_EPS)) * g
    lane = lax.broadcasted_iota(jnp.int32, x.shape, 1)
    first = (lane % SWA_HEAD_DIM) < (SWA_HEAD_DIM // 2)
    half = SWA_HEAD_DIM // 2
    partner = jnp.where(first, pltpu.roll(y, LANES - half, axis=1), pltpu.roll(y, half, axis=1))
    return y * cos + partner * sin


def _swa_kernel(sink_ref, q_ref, kc_ref, kp_ref, vc_ref, vp_ref, cc_ref, sc_ref, cp_ref, sp_ref,
                qg_ref, kg_ref, o_ref):
    blk = SWA_WINDOW
    n = pl.program_id(1)
    lane = lax.broadcasted_iota(jnp.int32, (1, LANES), 1)
    lo = lane < SWA_HEAD_DIM
    scale = SWA_HEAD_DIM ** -0.5

    cos_c, sin_c = cc_ref[...], sc_ref[...]
    kk = jnp.concatenate(
        [_swa_norm_rope(kp_ref[...].astype(F32), kg_ref[...], cp_ref[...], sp_ref[...], lo),
         _swa_norm_rope(kc_ref[...].astype(F32), kg_ref[...], cos_c, sin_c, lo)], axis=0)
    vv = jnp.concatenate([vp_ref[...], vc_ref[...]], axis=0).astype(F32)
    kk_sw = pltpu.roll(kk, SWA_HEAD_DIM, axis=1)
    vv_sw = pltpu.roll(vv, SWA_HEAD_DIM, axis=1)
    k_dup = [jnp.where(lo, kk, kk_sw).astype(BF16), jnp.where(lo, kk_sw, kk).astype(BF16)]
    v_src = [jnp.where(lo, vv, vv_sw), jnp.where(lo, vv_sw, vv)]
    v_lo = [jnp.where(lo, v, 0.0).astype(BF16) for v in v_src]
    v_hi = [jnp.where(lo, 0.0, v).astype(BF16) for v in v_src]

    qi = lax.broadcasted_iota(jnp.int32, (blk, 2 * blk), 0)
    si = lax.broadcasted_iota(jnp.int32, (blk, 2 * blk), 1)
    rel = qi + blk - si
    valid = (rel >= 0) & (rel < SWA_WINDOW) & ((si >= blk) | (n > 0))

    def attend(qm, kd, vm, sink):
        s = jnp.where(valid, _dot_nt(qm.astype(BF16), kd) * scale, MASK_VALUE)
        mx = jnp.maximum(jnp.max(s, axis=-1, keepdims=True), sink)
        e = jnp.exp(s - mx)
        den = jnp.sum(e, axis=-1, keepdims=True) + jnp.exp(sink - mx)
        return _dot(e.astype(BF16), vm) / den

    group = SWA_Q_HEADS // SWA_KV_HEADS
    for p in range(SWA_Q_HEADS // 2):
        sl = slice(p * LANES, (p + 1) * LANES)
        kv = (2 * p) // group
        qr = _swa_norm_rope(q_ref[:, sl].astype(F32), qg_ref[...], cos_c, sin_c, lo)
        out = (attend(jnp.where(lo, qr, 0.0), k_dup[kv], v_lo[kv], sink_ref[2 * p])
               + attend(jnp.where(lo, 0.0, qr), k_dup[kv], v_hi[kv], sink_ref[2 * p + 1]))
        o_ref[:, sl] = out.astype(BF16)


def _swa(proj, cos, sin, q_g, k_g, sinks, *, batch):
    m = proj.shape[0]
    blk = SWA_WINDOW
    nb = (m // batch) // blk
    qw = SWA_Q_HEADS * SWA_HEAD_DIM
    kcol = qw // LANES
    cur = lambda b, n: b * nb + n
    prev = lambda b, n: b * nb + jnp.maximum(n - 1, 0)
    tile2 = lambda g: jnp.concatenate([g, g]).reshape(1, LANES)
    return pl.pallas_call(
        _swa_kernel,
        grid=(batch, nb),
        in_specs=[pl.BlockSpec(memory_space=pltpu.SMEM),
                  pl.BlockSpec((blk, qw), lambda b, n: (cur(b, n), 0)),
                  pl.BlockSpec((blk, LANES), lambda b, n: (cur(b, n), kcol)),
                  pl.BlockSpec((blk, LANES), lambda b, n: (prev(b, n), kcol)),
                  pl.BlockSpec((blk, LANES), lambda b, n: (cur(b, n), kcol + 1)),
                  pl.BlockSpec((blk, LANES), lambda b, n: (prev(b, n), kcol + 1)),
                  pl.BlockSpec((blk, LANES), lambda b, n: (cur(b, n), 0)),
                  pl.BlockSpec((blk, LANES), lambda b, n: (cur(b, n), 0)),
                  pl.BlockSpec((blk, LANES), lambda b, n: (prev(b, n), 0)),
                  pl.BlockSpec((blk, LANES), lambda b, n: (prev(b, n), 0)),
                  pl.BlockSpec((1, LANES), lambda b, n: (0, 0)),
                  pl.BlockSpec((1, LANES), lambda b, n: (0, 0))],
        out_specs=pl.BlockSpec((blk, qw), lambda b, n: (cur(b, n), 0)),
        out_shape=jax.ShapeDtypeStruct((m, qw), BF16),
        compiler_params=_cparams("parallel", "parallel"),
        name="swa",
    )(sinks, proj, proj, proj, proj, proj, cos, sin, cos, sin, tile2(q_g), tile2(k_g))


def _conv_kernel(a_ref, gate_ref, cw_ref, cb_ref, lg_ref, lb_ref, o_ref, u_ref, y_ref, *, tile):
    t = pl.program_id(1)
    halo = CONV_HALO

    @pl.when(t == 0)
    def _():
        u_ref[0:halo, :] = jnp.zeros((halo, u_ref.shape[1]), F32)

    @pl.when(t > 0)
    def _():
        u_ref[0:halo, :] = u_ref[tile:tile + halo, :]

    u_ref[halo:halo + tile, :] = a_ref[...].astype(F32) * _sigmoid(gate_ref[...].astype(F32))

    base = halo - (CONV_WIDTH - 1)
    width = u_ref.shape[1]
    for c0 in range(0, width, LANES):
        for r0 in range(0, tile, CONV_ROWS):
            acc = jnp.zeros((CONV_ROWS, LANES), F32)
            for w in range(CONV_WIDTH):
                acc = acc + (u_ref[base + r0 + w:base + r0 + w + CONV_ROWS, c0:c0 + LANES]
                             * cw_ref[w:w + 1, c0:c0 + LANES])
            y_ref[r0:r0 + CONV_ROWS, c0:c0 + LANES] = acc

    y = y_ref[...] + cb_ref[...]
    mu = jnp.mean(y, axis=-1, keepdims=True)
    yc = y - mu
    var = jnp.mean(yc * yc, axis=-1, keepdims=True)
    z = yc * lax.rsqrt(var + LN_EPS) * lg_ref[...] + lb_ref[...]
    o_ref[...] = (z * _sigmoid(z)).astype(BF16)


def _conv(proj, conv_w, conv_b, ln_g, ln_b, *, batch):
    m = proj.shape[0]
    c = conv_w.shape[1]
    tile = CONV_TILE
    nt = (m // batch) // tile
    vec = lambda: pl.BlockSpec((1, c), lambda b, t: (0, 0))
    return pl.pallas_call(
        functools.partial(_conv_kernel, tile=tile),
        grid=(batch, nt),
        in_specs=[pl.BlockSpec((tile, c), lambda b, t: (b * nt + t, 0)),
                  pl.BlockSpec((tile, c), lambda b, t: (b * nt + t, 1)),
                  pl.BlockSpec((CONV_WIDTH, c), lambda b, t: (0, 0)),
                  vec(), vec(), vec()],
        out_specs=pl.BlockSpec((tile, c), lambda b, t: (b * nt + t, 0)),
        out_shape=jax.ShapeDtypeStruct((m, c), BF16),
        scratch_shapes=[pltpu.VMEM((CONV_HALO + tile, c), F32), pltpu.VMEM((tile, c), F32)],
        compiler_params=_cparams("parallel", "arbitrary"),
        name="conformer_conv",
    )(proj, proj, conv_w, conv_b.reshape(1, c), ln_g.reshape(1, c), ln_b.reshape(1, c))


def kernel(x, mem, positions, norm_mix_g, norm_mlp_g, mem_norm_g, xa_w_kv, xa_q_norm_g, xa_k_norm_g, mlp_w_up, mlp_w_down, hgrn_lb_logits, a_w_in, a_o_norm_g, a_w_out, b_w_in, b_q_norm_g, b_k_norm_g, b_sinks, b_w_out, c_w_in, c_conv_w, c_conv_b, c_ln_g, c_ln_b, c_w_out):
    batch, seq, d = x.shape
    depth = norm_mix_g.shape[0]
    kinds = tuple(i % 3 for i in range(depth))
    xs = x.reshape(batch * seq, d)

    kx, vx = _mem_kv(mem, mem_norm_g, xa_w_kv.astype(BF16), xa_k_norm_g)
    if 1 in kinds:
        cos, sin = _rope_tables(positions)

    for layer in range(depth):
        kind = kinds[layer]
        idx = kinds[:layer].count(kind)
        if kind == 0:
            proj, qx = _in_proj(xs, norm_mix_g[layer], a_w_in[idx].astype(BF16), head_major=True)
            mix = _hgrn(proj, hgrn_lb_logits, a_o_norm_g[idx], layer, batch=batch)
            w_out = a_w_out[idx]
        elif kind == 1:
            proj, qx = _in_proj(xs, norm_mix_g[layer], b_w_in[idx].astype(BF16), head_major=False)
            mix = _swa(proj, cos, sin, b_q_norm_g[idx], b_k_norm_g[idx], b_sinks[idx], batch=batch)
            w_out = b_w_out[idx]
        else:
            proj, qx = _in_proj(xs, norm_mix_g[layer], c_w_in[idx].astype(BF16), head_major=False)
            mix = _conv(proj, c_conv_w[idx], c_conv_b[idx], c_ln_g[idx], c_ln_b[idx], batch=batch)
            w_out = c_w_out[idx]
        xs = _out_proj(xs, mix, qx, kx, vx, xa_q_norm_g[layer], w_out.astype(BF16), layer,
                       batch=batch, head_major=(kind == 0))
        xs = _mlp(xs, norm_mlp_g[layer], mlp_w_up[layer].astype(BF16), mlp_w_down[layer].astype(BF16))
    return xs.reshape(batch, seq, d)
```

```python
import functools

import numpy as np
import jax
import jax.numpy as jnp
from jax import lax
from jax.experimental import pallas as pl
from jax.experimental.pallas import tpu as pltpu

F32 = jnp.float32
BF16 = jnp.bfloat16

RMS_EPS = 1e-6
LN_EPS = 1e-5
MASK_VALUE = -1e30
MIN_FORGET = 1e-20
ROPE_THETA = 10000.0

LANES = 128
SUBLANES = 8

N_MEM = 256
XA_HEADS = 4
XA_HEAD_DIM = 128
XA_WIDTH = XA_HEADS * XA_HEAD_DIM
HGRN_HEADS = 8
HGRN_HEAD_DIM = 128
HGRN_TILE = 128
HGRN_UNROLL = 8
SWA_HEAD_DIM = 64
SWA_Q_HEADS = 16
SWA_KV_HEADS = 2
SWA_WINDOW = 128
CONV_WIDTH = 31
CONV_HALO = 32
CONV_TILE = 128
CONV_ROWS = 64

VMEM_LIMIT = 56 * 1024 * 1024


def _cparams(*sem):
    return pltpu.CompilerParams(dimension_semantics=sem, vmem_limit_bytes=VMEM_LIMIT)


def _dot(a, b):
    return jnp.dot(a, b, preferred_element_type=F32)


def _dot_nt(a, b):
    return lax.dot_general(a, b, (((1,), (1,)), ((), ())), preferred_element_type=F32)


def _dot_tn(a, b):
    return lax.dot_general(a, b, (((0,), (0,)), ((), ())), preferred_element_type=F32)


def _rms(x, g):
    return x * lax.rsqrt(jnp.mean(x * x, axis=-1, keepdims=True) + RMS_EPS) * g


def _sigmoid(x):
    return 1.0 / (1.0 + jnp.exp(-x))


def _in_proj_kernel(x_ref, g_ref, w_ref, mix_ref, qx_ref, *, n_mix, tn, head_major):
    h = _rms(x_ref[...], g_ref[...]).astype(BF16)
    for c0 in range(0, n_mix, tn):
        r = _dot(h, w_ref[:, c0:c0 + tn]).astype(BF16)
        if head_major:
            for j in range(tn // LANES):
                mix_ref[c0 // LANES + j] = r[:, j * LANES:(j + 1) * LANES]
        else:
            mix_ref[:, c0:c0 + tn] = r
    qx_ref[...] = _dot(h, w_ref[:, n_mix:]).astype(BF16)


def _in_proj(x, g, w, *, head_major, tm=512):
    m, d = x.shape
    n = w.shape[1]
    n_mix = n - XA_WIDTH
    tn = 512 if n_mix % 512 == 0 else 256
    assert m % tm == 0 and n_mix % tn == 0
    if head_major:
        mix_shape = jax.ShapeDtypeStruct((n_mix // LANES, m, LANES), BF16)
        mix_spec = pl.BlockSpec((n_mix // LANES, tm, LANES), lambda i: (0, i, 0))
    else:
        mix_shape = jax.ShapeDtypeStruct((m, n_mix), BF16)
        mix_spec = pl.BlockSpec((tm, n_mix), lambda i: (i, 0))
    return pl.pallas_call(
        functools.partial(_in_proj_kernel, n_mix=n_mix, tn=tn, head_major=head_major),
        grid=(m // tm,),
        in_specs=[pl.BlockSpec((tm, d), lambda i: (i, 0)),
                  pl.BlockSpec((1, d), lambda i: (0, 0)),
                  pl.BlockSpec((d, n), lambda i: (0, 0))],
        out_specs=[mix_spec, pl.BlockSpec((tm, XA_WIDTH), lambda i: (i, 0))],
        out_shape=[mix_shape, jax.ShapeDtypeStruct((m, XA_WIDTH), BF16)],
        compiler_params=_cparams("parallel"),
        name="in_proj",
    )(x, g.reshape(1, d), w)


def _mem_kv_kernel(mem_ref, g_ref, w_ref, kg_ref, k_ref, v_ref):
    memn = _rms(mem_ref[...], g_ref[...]).astype(BF16)
    kv = _dot(memn, w_ref[...])
    kg = kg_ref[...]
    for h in range(XA_HEADS):
        sl = slice(h * XA_HEAD_DIM, (h + 1) * XA_HEAD_DIM)
        k_ref[:, sl] = _rms(kv[:, sl], kg).astype(BF16)
    v_ref[...] = kv[:, XA_WIDTH:].astype(BF16)


def _mem_kv(mem, mem_norm_g, w_kv, k_g):
    b, nm, d = mem.shape
    depth = w_kv.shape[0]
    kv_shape = jax.ShapeDtypeStruct((depth, b, nm, XA_WIDTH), BF16)
    kv_spec = pl.BlockSpec((None, None, nm, XA_WIDTH), lambda l, i: (l, i, 0, 0))
    return pl.pallas_call(
        _mem_kv_kernel,
        grid=(depth, b),
        in_specs=[pl.BlockSpec((None, nm, d), lambda l, i: (i, 0, 0)),
                  pl.BlockSpec((None, 1, d), lambda l, i: (l, 0, 0)),
                  pl.BlockSpec((None, d, 2 * XA_WIDTH), lambda l, i: (l, 0, 0)),
                  pl.BlockSpec((None, 1, XA_HEAD_DIM), lambda l, i: (l, 0, 0))],
        out_specs=[kv_spec, kv_spec],
        out_shape=[kv_shape, kv_shape],
        compiler_params=_cparams("parallel", "parallel"),
        name="mem_kv",
    )(mem, mem_norm_g.reshape(depth, 1, d), w_kv, k_g.reshape(depth, 1, XA_HEAD_DIM))


def _out_proj_kernel(x_ref, mix_ref, qx_ref, k_ref, v_ref, qg_ref, w_ref, o_ref, *, head_major):
    qg = qg_ref[...]
    scale = XA_HEAD_DIM ** -0.5
    xa = []
    for h in range(XA_HEADS):
        sl = slice(h * XA_HEAD_DIM, (h + 1) * XA_HEAD_DIM)
        qn = _rms(qx_ref[:, sl].astype(F32), qg).astype(BF16)
        s = _dot_nt(qn, k_ref[:, sl]) * scale
        e = jnp.exp(s - jnp.max(s, axis=-1, keepdims=True))
        den = jnp.sum(e, axis=-1, keepdims=True)
        xa.append((_dot(e.astype(BF16), v_ref[:, sl]) / den).astype(BF16))
    xa = jnp.concatenate(xa, axis=-1)
    if head_major:
        mix = jnp.concatenate([mix_ref[h] for h in range(mix_ref.shape[0])], axis=-1)
    else:
        mix = mix_ref[...]
    d_mix = mix.shape[-1]
    o_ref[...] = x_ref[...] + _dot(mix, w_ref[:d_mix, :]) + _dot(xa, w_ref[d_mix:, :])


def _out_proj(x, mix, qx, kx, vx, q_g, w_out, layer, *, batch, head_major, tm=512):
    m, d = x.shape
    t = m // batch
    nt = t // tm
    nm = kx.shape[2]
    if head_major:
        nh = mix.shape[0]
        mix_spec = pl.BlockSpec((nh, tm, LANES), lambda b, i: (0, b * nt + i, 0))
    else:
        mix_spec = pl.BlockSpec((tm, mix.shape[1]), lambda b, i: (b * nt + i, 0))
    kv_spec = pl.BlockSpec((None, None, nm, XA_WIDTH), lambda b, i: (layer, b, 0, 0))
    row_spec = lambda w: pl.BlockSpec((tm, w), lambda b, i: (b * nt + i, 0))
    return pl.pallas_call(
        functools.partial(_out_proj_kernel, head_major=head_major),
        grid=(batch, nt),
        in_specs=[row_spec(d), mix_spec, row_spec(XA_WIDTH), kv_spec, kv_spec,
                  pl.BlockSpec((1, XA_HEAD_DIM), lambda b, i: (0, 0)),
                  pl.BlockSpec(w_out.shape, lambda b, i: (0, 0))],
        out_specs=row_spec(d),
        out_shape=jax.ShapeDtypeStruct((m, d), F32),
        compiler_params=_cparams("parallel", "parallel"),
        name="out_proj",
    )(x, mix, qx, kx, vx, q_g.reshape(1, XA_HEAD_DIM), w_out)


def _mlp_kernel(x_ref, g_ref, wu_ref, wd_ref, o_ref, h_ref, acc_ref):
    k = pl.program_id(1)

    @pl.when(k == 0)
    def _():
        h_ref[...] = _rms(x_ref[...], g_ref[...]).astype(BF16)
        acc_ref[...] = x_ref[...]

    u = jnp.maximum(_dot(h_ref[...], wu_ref[...]), 0.0)
    acc_ref[...] += _dot((u * u).astype(BF16), wd_ref[...])

    @pl.when(k == pl.num_programs(1) - 1)
    def _():
        o_ref[...] = acc_ref[...]


def _mlp(x, g, w_up, w_down, *, tm=1024, th=1024):
    m, d = x.shape
    hid = w_up.shape[1]
    assert m % tm == 0 and hid % th == 0
    return pl.pallas_call(
        _mlp_kernel,
        grid=(m // tm, hid // th),
        in_specs=[pl.BlockSpec((tm, d), lambda i, k: (i, 0)),
                  pl.BlockSpec((1, d), lambda i, k: (0, 0)),
                  pl.BlockSpec((d, th), lambda i, k: (0, k)),
                  pl.BlockSpec((th, d), lambda i, k: (k, 0))],
        out_specs=pl.BlockSpec((tm, d), lambda i, k: (i, 0)),
        out_shape=jax.ShapeDtypeStruct((m, d), F32),
        scratch_shapes=[pltpu.VMEM((tm, d), BF16), pltpu.VMEM((tm, d), F32)],
        compiler_params=_cparams("parallel", "arbitrary"),
        name="mlp",
    )(x, g.reshape(1, d), w_up, w_down)


def _hgrn_levels(tile):
    levels = []
    s = 1
    while s < tile:
        levels.append(s)
        s *= 2
    return tuple(levels)


def _hgrn_pair_masks(tile):
    levels = _hgrn_levels(tile)
    i = np.arange(tile)[:, None]
    j = np.arange(tile)[None, :]
    return np.stack([((i // s == j // s + 1) & ((j // s) % 2 == 0)) for s in levels]).astype(np.float32)


def _hgrn_exponent_matrix(tile):
    levels = _hgrn_levels(tile)
    w = np.zeros((len(levels) * tile, tile), np.float32)
    for l, s in enumerate(levels[1:]):
        for i in range(tile):
            mid = (i // (2 * s)) * 2 * s + s - 1
            if (i // s) % 2 == 1:
                w[l * tile + i, mid + 1:i + 1] = 1.0
            else:
                w[l * tile + i, i + 1:mid + 1] = 1.0
    for i in range(tile):
        w[(len(levels) - 1) * tile + i, :i + 1] = 1.0
    return np.concatenate([w, w], axis=1)


def _hgrn_kernel(q_ref, f_ref, i_ref, g_ref, lg_ref, og_ref, w_ref, pm_ref, o_ref, st_ref, *,
                 layer, tile):
    levels = _hgrn_levels(tile)
    n_lev = len(levels)

    @pl.when(pl.program_id(1) == 0)
    def _():
        st_ref[...] = jnp.zeros_like(st_ref)

    row = lax.broadcasted_iota(jnp.int32, (tile, LANES), 0)

    heads = range(HGRN_HEADS)

    def gates(h):
        lg = lg_ref[h]
        e = jnp.exp(lg - jnp.max(lg, axis=0, keepdims=True))
        lb_num = jnp.zeros((1, LANES), F32)
        for r in range(1, layer + 1):
            lb_num = lb_num + e[r:r + 1, :]
        lb = lb_num / jnp.sum(e, axis=0, keepdims=True)
        f = lb + (1.0 - lb) * _sigmoid(f_ref[h].astype(F32))
        fc = jnp.maximum(f, MIN_FORGET)
        logf = jnp.log(fc)
        hi = logf.astype(BF16)
        lo = (logf - hi.astype(F32)).astype(BF16)
        ex = _dot(w_ref[...], jnp.concatenate([hi, lo], axis=0))
        return 1.0 - f, fc, ex

    def scaled(h, k, fc, ex):
        q = q_ref[h].astype(F32)
        xs = [jnp.where((row & 1) != 0, q * jnp.minimum(fc, 1.0), k).astype(BF16)]
        for l, s in enumerate(levels[1:]):
            el = jnp.exp(jnp.minimum(ex[l * tile:(l + 1) * tile, :], 0.0))
            xs.append((jnp.where((row & s) != 0, q, k) * el).astype(BF16))
        b = ex[(n_lev - 1) * tile:, :]
        b_last = b[tile - 1:tile, :]
        qh = (q * jnp.exp(b)).astype(BF16)
        kh = (k * jnp.exp(b_last - b)).astype(BF16)
        diag = jnp.sum(q * k, axis=-1, keepdims=True)
        return xs, qh, kh, jnp.exp(b_last), diag

    def pair_weights(xs):
        attn = _dot_nt(xs[0], xs[0]) * pm_ref[0]
        for l in range(1, n_lev):
            attn = attn + _dot_nt(xs[l], xs[l]) * pm_ref[l]
        return attn.astype(BF16)

    def output(h, attn, qh, kh, dec, diag):
        v = i_ref[h]
        st = st_ref[h]
        o = diag * v.astype(F32) + _dot(attn, v) + _dot_nt(qh, st.astype(BF16))
        st_ref[h] = dec * st + _dot_tn(v, kh)
        gate = g_ref[h].astype(F32)
        o_ref[h] = (_rms(o, og_ref[h]) * (gate * _sigmoid(gate))).astype(BF16)

    s1 = [gates(h) for h in heads]
    s2 = [scaled(h, *s1[h]) for h in heads]
    s3 = [pair_weights(s2[h][0]) for h in heads]
    for h in heads:
        output(h, s3[h], *s2[h][1:])


def _hgrn(proj_hm, lb_logits, o_norm_g, layer, *, batch):
    _, m, _ = proj_hm.shape
    tile = HGRN_TILE
    nt = (m // batch) // tile
    depth = lb_logits.shape[0]
    wmat = jnp.asarray(_hgrn_exponent_matrix(tile), BF16)
    pmask = jnp.asarray(_hgrn_pair_masks(tile), F32)
    lg =lb_logits.reshape(depth, HGRN_HEADS, HGRN_HEAD_DIM).transpose(1, 0, 2)
    sec = lambda s: pl.BlockSpec((HGRN_HEADS, tile, LANES), lambda b, t: (s, b * nt + t, 0))
    return pl.pallas_call(
        functools.partial(_hgrn_kernel, layer=layer, tile=tile),
        grid=(batch, nt),
        in_specs=[sec(0), sec(1), sec(2), sec(3),
                  pl.BlockSpec((HGRN_HEADS, depth, LANES), lambda b, t: (0, 0, 0)),
                  pl.BlockSpec((HGRN_HEADS, 1, LANES), lambda b, t: (0, 0, 0)),
                  pl.BlockSpec(wmat.shape, lambda b, t: (0, 0)),
                  pl.BlockSpec(pmask.shape, lambda b, t: (0, 0, 0))],
        out_specs=pl.BlockSpec((HGRN_HEADS, tile, LANES), lambda b, t: (0, b * nt + t, 0)),
        out_shape=jax.ShapeDtypeStruct((HGRN_HEADS, m, LANES), BF16),
        scratch_shapes=[pltpu.VMEM((HGRN_HEADS, HGRN_HEAD_DIM, HGRN_HEAD_DIM), F32)],
        compiler_params=_cparams("parallel", "arbitrary"),
        name="hgrn2",
    )(proj_hm, proj_hm, proj_hm, proj_hm, lg,
      o_norm_g.reshape(HGRN_HEADS, 1, HGRN_HEAD_DIM), wmat, pmask)


def _rope_table_kernel(pos_ref, cos_ref, sin_ref):
    lane = lax.broadcasted_iota(jnp.int32, (1, LANES), 1)
    half = SWA_HEAD_DIM // 2
    idx = (lane % half).astype(F32)
    inv_freq = jnp.exp(idx * (-np.log(ROPE_THETA) / half))
    ang = pos_ref[...].astype(F32) * inv_freq
    first = (lane % SWA_HEAD_DIM) < half
    cos_ref[...] = jnp.cos(ang)
    sin_ref[...] = jnp.where(first, -1.0, 1.0) * jnp.sin(ang)


def _rope_tables(positions, tm=1024):
    m = positions.size
    shape = jax.ShapeDtypeStruct((m, LANES), F32)
    spec = pl.BlockSpec((tm, LANES), lambda i: (i, 0))
    return pl.pallas_call(
        _rope_table_kernel,
        grid=(m // tm,),
        in_specs=[pl.BlockSpec((tm, 1), lambda i: (i, 0))],
        out_specs=[spec, spec],
        out_shape=[shape, shape],
        compiler_params=_cparams("parallel"),
        name="rope_tables",
    )(positions.reshape(m, 1))


def _swa_norm_rope(x, g, cos, sin, lo):
    sq = x * x
    ms_lo = jnp.sum(jnp.where(lo, sq, 0.0), axis=-1, keepdims=True) * (1.0 / SWA_HEAD_DIM)
    ms_hi = jnp.sum(jnp.where(lo, 0.0, sq), axis=-1, keepdims=True) * (1.0 / SWA_HEAD_DIM)
    y = x * jnp.where(lo, lax.rsqrt(ms_lo + RMS_EPS), lax.rsqrt(ms_hi + RMS_EPS)) * g
    lane = lax.broadcasted_iota(jnp.int32, x.shape, 1)
    first = (lane % SWA_HEAD_DIM) < (SWA_HEAD_DIM // 2)
    half = SWA_HEAD_DIM // 2
    partner = jnp.where(first, pltpu.roll(y, LANES - half, axis=1), pltpu.roll(y, half, axis=1))
    return y * cos + partner * sin


def _swa_kernel(sink_ref, q_ref, kc_ref, kp_ref, vc_ref, vp_ref, cc_ref, sc_ref, cp_ref, sp_ref,
                qg_ref, kg_ref, o_ref):
    blk = SWA_WINDOW
    n = pl.program_id(1)
    lane = lax.broadcasted_iota(jnp.int32, (1, LANES), 1)
    lo = lane < SWA_HEAD_DIM
    scale = SWA_HEAD_DIM ** -0.5

    cos_c, sin_c = cc_ref[...], sc_ref[...]
    kk = jnp.concatenate(
        [_swa_norm_rope(kp_ref[...].astype(F32), kg_ref[...], cp_ref[...], sp_ref[...], lo),
         _swa_norm_rope(kc_ref[...].astype(F32), kg_ref[...], cos_c, sin_c, lo)], axis=0)
    vv = jnp.concatenate([vp_ref[...], vc_ref[...]], axis=0).astype(F32)
    kk_sw = pltpu.roll(kk, SWA_HEAD_DIM, axis=1)
    vv_sw = pltpu.roll(vv, SWA_HEAD_DIM, axis=1)
    k_dup = [jnp.where(lo, kk, kk_sw).astype(BF16), jnp.where(lo, kk_sw, kk).astype(BF16)]
    v_src = [jnp.where(lo, vv, vv_sw), jnp.where(lo, vv_sw, vv)]
    v_lo = [jnp.where(lo, v, 0.0).astype(BF16) for v in v_src]
    v_hi = [jnp.where(lo, 0.0, v).astype(BF16) for v in v_src]

    qi = lax.broadcasted_iota(jnp.int32, (blk, 2 * blk), 0)
    si = lax.broadcasted_iota(jnp.int32, (blk, 2 * blk), 1)
    rel = qi + blk - si
    valid = (rel >= 0) & (rel < SWA_WINDOW) & ((si >= blk) | (n > 0))

    def attend(qm, kd, vm, sink):
        s = jnp.where(valid, _dot_nt(qm.astype(BF16), kd) * scale, MASK_VALUE)
        mx = jnp.maximum(jnp.max(s, axis=-1, keepdims=True), sink)
        e = jnp.exp(s - mx)
        den = jnp.sum(e, axis=-1, keepdims=True) + jnp.exp(sink - mx)
        return _dot(e.astype(BF16), vm) / den

    group = SWA_Q_HEADS // SWA_KV_HEADS
    for p in range(SWA_Q_HEADS // 2):
        sl = slice(p * LANES, (p + 1) * LANES)
        kv = (2 * p) // group
        qr = _swa_norm_rope(q_ref[:, sl].astype(F32), qg_ref[...], cos_c, sin_c, lo)
        out = (attend(jnp.where(lo, qr, 0.0), k_dup[kv], v_lo[kv], sink_ref[2 * p])
               + attend(jnp.where(lo, 0.0, qr), k_dup[kv], v_hi[kv], sink_ref[2 * p + 1]))
        o_ref[:, sl] = out.astype(BF16)


def _swa(proj, cos, sin, q_g, k_g, sinks, *, batch):
    m = proj.shape[0]
    blk = SWA_WINDOW
    nb = (m // batch) // blk
    qw = SWA_Q_HEADS * SWA_HEAD_DIM
    kcol = qw // LANES
    cur = lambda b, n: b * nb + n
    prev = lambda b, n: b * nb + jnp.maximum(n - 1, 0)
    tile2 = lambda g: jnp.concatenate([g, g]).reshape(1, LANES)
    return pl.pallas_call(
        _swa_kernel,
        grid=(batch, nb),
        in_specs=[pl.BlockSpec(memory_space=pltpu.SMEM),
                  pl.BlockSpec((blk, qw), lambda b, n: (cur(b, n), 0)),
                  pl.BlockSpec((blk, LANES), lambda b, n: (cur(b, n), kcol)),
                  pl.BlockSpec((blk, LANES), lambda b, n: (prev(b, n), kcol)),
                  pl.BlockSpec((blk, LANES), lambda b, n: (cur(b, n), kcol + 1)),
                  pl.BlockSpec((blk, LANES), lambda b, n: (prev(b, n), kcol + 1)),
                  pl.BlockSpec((blk, LANES), lambda b, n: (cur(b, n), 0)),
                  pl.BlockSpec((blk, LANES), lambda b, n: (cur(b, n), 0)),
                  pl.BlockSpec((blk, LANES), lambda b, n: (prev(b, n), 0)),
                  pl.BlockSpec((blk, LANES), lambda b, n: (prev(b, n), 0)),
                  pl.BlockSpec((1, LANES), lambda b, n: (0, 0)),
                  pl.BlockSpec((1, LANES), lambda b, n: (0, 0))],
        out_specs=pl.BlockSpec((blk, qw), lambda b, n: (cur(b, n), 0)),
        out_shape=jax.ShapeDtypeStruct((m, qw), BF16),
        compiler_params=_cparams("parallel", "parallel"),
        name="swa",
    )(sinks, proj, proj, proj, proj, proj, cos, sin, cos, sin, tile2(q_g), tile2(k_g))


def _conv_kernel(a_ref, gate_ref, cw_ref, cb_ref, lg_ref, lb_ref, o_ref, u_ref, y_ref, *, tile):
    t = pl.program_id(1)
    halo = CONV_HALO

    @pl.when(t == 0)
    def _():
        u_ref[0:halo, :] = jnp.zeros((halo, u_ref.shape[1]), F32)

    @pl.when(t > 0)
    def _():
        u_ref[0:halo, :] = u_ref[tile:tile + halo, :]

    u_ref[halo:halo + tile, :] = a_ref[...].astype(F32) * _sigmoid(gate_ref[...].astype(F32))

    base = halo - (CONV_WIDTH - 1)
    width = u_ref.shape[1]
    for c0 in range(0, width, LANES):
        for r0 in range(0, tile, CONV_ROWS):
            acc = jnp.zeros((CONV_ROWS, LANES), F32)
            for w in range(CONV_WIDTH):
                acc = acc + (u_ref[base + r0 + w:base + r0 + w + CONV_ROWS, c0:c0 + LANES]
                             * cw_ref[w:w + 1, c0:c0 + LANES])
            y_ref[r0:r0 + CONV_ROWS, c0:c0 + LANES] = acc

    y = y_ref[...] + cb_ref[...]
    mu = jnp.mean(y, axis=-1, keepdims=True)
    yc = y - mu
    var = jnp.mean(yc * yc, axis=-1, keepdims=True)
    z = yc * lax.rsqrt(var + LN_EPS) * lg_ref[...] + lb_ref[...]
    o_ref[...] = (z * _sigmoid(z)).astype(BF16)


def _conv(proj, conv_w, conv_b, ln_g, ln_b, *, batch):
    m = proj.shape[0]
    c = conv_w.shape[1]
    tile = CONV_TILE
    nt = (m // batch) // tile
    vec = lambda: pl.BlockSpec((1, c), lambda b, t: (0, 0))
    return pl.pallas_call(
        functools.partial(_conv_kernel, tile=tile),
        grid=(batch, nt),
        in_specs=[pl.BlockSpec((tile, c), lambda b, t: (b * nt + t, 0)),
                  pl.BlockSpec((tile, c), lambda b, t: (b * nt + t, 1)),
                  pl.BlockSpec((CONV_WIDTH, c), lambda b, t: (0, 0)),
                  vec(), vec(), vec()],
        out_specs=pl.BlockSpec((tile, c), lambda b, t: (b * nt + t, 0)),
        out_shape=jax.ShapeDtypeStruct((m, c), BF16),
        scratch_shapes=[pltpu.VMEM((CONV_HALO + tile, c), F32), pltpu.VMEM((tile, c), F32)],
        compiler_params=_cparams("parallel", "arbitrary"),
        name="conformer_conv",
    )(proj, proj, conv_w, conv_b.reshape(1, c), ln_g.reshape(1, c), ln_b.reshape(1, c))


def kernel(x, mem, positions, norm_mix_g, norm_mlp_g, mem_norm_g, xa_w_kv, xa_q_norm_g, xa_k_norm_g, mlp_w_up, mlp_w_down, hgrn_lb_logits, a_w_in, a_o_norm_g, a_w_out, b_w_in, b_q_norm_g, b_k_norm_g, b_sinks, b_w_out, c_w_in, c_conv_w, c_conv_b, c_ln_g, c_ln_b, c_w_out):
    batch, seq, d = x.shape
    depth = norm_mix_g.shape[0]
    kinds = tuple(i % 3 for i in range(depth))
    xs = x.reshape(batch * seq, d)

    kx, vx = _mem_kv(mem, mem_norm_g, xa_w_kv.astype(BF16), xa_k_norm_g)
    if 1 in kinds:
        cos, sin = _rope_tables(positions)

    for layer in range(depth):
        kind = kinds[layer]
        idx = kinds[:layer].count(kind)
        if kind == 0:
            proj, qx = _in_proj(xs, norm_mix_g[layer], a_w_in[idx].astype(BF16), head_major=True)
            mix = _hgrn(proj, hgrn_lb_logits, a_o_norm_g[idx], layer, batch=batch)
            w_out = a_w_out[idx]
        elif kind == 1:
            proj, qx = _in_proj(xs, norm_mix_g[layer], b_w_in[idx].astype(BF16), head_major=False)
            mix = _swa(proj, cos, sin, b_q_norm_g[idx], b_k_norm_g[idx], b_sinks[idx], batch=batch)
            w_out = b_w_out[idx]
        else:
            proj, qx = _in_proj(xs, norm_mix_g[layer], c_w_in[idx].astype(BF16), head_major=False)
            mix = _conv(proj, c_conv_w[idx], c_conv_b[idx], c_ln_g[idx], c_ln_b[idx], batch=batch)
            w_out = c_w_out[idx]
        xs = _out_proj(xs, mix, qx, kx, vx, xa_q_norm_g[layer], w_out.astype(BF16), layer,
                       batch=batch, head_major=(kind == 0))
        xs = _mlp(xs, norm_mlp_g[layer], mlp_w_up[layer].astype(BF16), mlp_w_down[layer].astype(BF16))
    return xs.reshape(batch, seq, d)
```

```python
import functools

import numpy as np
import jax
import jax.numpy as jnp
from jax import lax
from jax.experimental import pallas as pl
from jax.experimental.pallas import tpu as pltpu

F32 = jnp.float32
BF16 = jnp.bfloat16

RMS_EPS = 1e-6
LN_EPS = 1e-5
MASK_VALUE = -1e30
MIN_FORGET = 1e-20
ROPE_THETA = 10000.0

LANES = 128
SUBLANES = 8

N_MEM = 256
XA_HEADS = 4
XA_HEAD_DIM = 128
XA_WIDTH = XA_HEADS * XA_HEAD_DIM
HGRN_HEADS = 8
HGRN_HEAD_DIM = 128
HGRN_TILE = 128
OUT_PROJ_SUB = 256
SWA_HEAD_DIM = 64
SWA_Q_HEADS = 16
SWA_KV_HEADS = 2
SWA_WINDOW = 128
CONV_WIDTH = 31
CONV_HALO = 32
CONV_TILE = 128
CONV_ROWS = 64

VMEM_LIMIT = 56 * 1024 * 1024


def _cparams(*sem):
    return pltpu.CompilerParams(dimension_semantics=sem, vmem_limit_bytes=VMEM_LIMIT)


def _dot(a, b):
    return jnp.dot(a, b, preferred_element_type=F32)


def _dot_nt(a, b):
    return lax.dot_general(a, b, (((1,), (1,)), ((), ())), preferred_element_type=F32)


def _dot_tn(a, b):
    return lax.dot_general(a, b, (((0,), (0,)), ((), ())), preferred_element_type=F32)


def _rms(x, g):
    return x * lax.rsqrt(jnp.mean(x * x, axis=-1, keepdims=True) + RMS_EPS) * g


def _sigmoid(x):
    return 1.0 / (1.0 + jnp.exp(-x))


def _in_proj_kernel(x_ref, g_ref, w_ref, mix_ref, qx_ref, *, n_mix, tn, head_major):
    h = x_ref[...] if x_ref.dtype == BF16 else _rms(x_ref[...], g_ref[...]).astype(BF16)
    for c0 in range(0, n_mix, tn):
        r = _dot(h, w_ref[:, c0:c0 + tn]).astype(BF16)
        if head_major:
            for j in range(tn // LANES):
                mix_ref[c0 // LANES + j] = r[:, j * LANES:(j + 1) * LANES]
        else:
            mix_ref[:, c0:c0 + tn] = r
    qx_ref[...] = _dot(h, w_ref[:, n_mix:]).astype(BF16)


def _in_proj(x, g, w, *, head_major, tm=512):
    m, d = x.shape
    n = w.shape[1]
    n_mix = n - XA_WIDTH
    tn = 512 if n_mix % 512 == 0 else 256
    assert m % tm == 0 and n_mix % tn == 0
    if head_major:
        mix_shape = jax.ShapeDtypeStruct((n_mix // LANES, m, LANES), BF16)
        mix_spec = pl.BlockSpec((n_mix // LANES, tm, LANES), lambda i: (0, i, 0))
    else:
        mix_shape = jax.ShapeDtypeStruct((m, n_mix), BF16)
        mix_spec = pl.BlockSpec((tm, n_mix), lambda i: (i, 0))
    return pl.pallas_call(
        functools.partial(_in_proj_kernel, n_mix=n_mix, tn=tn, head_major=head_major),
        grid=(m // tm,),
        in_specs=[pl.BlockSpec((tm, d), lambda i: (i, 0)),
                  pl.BlockSpec((1, d), lambda i: (0, 0)),
                  pl.BlockSpec((d, n), lambda i: (0, 0))],
        out_specs=[mix_spec, pl.BlockSpec((tm, XA_WIDTH), lambda i: (i, 0))],
        out_shape=[mix_shape, jax.ShapeDtypeStruct((m, XA_WIDTH), BF16)],
        compiler_params=_cparams("parallel"),
        name="in_proj",
    )(x, g.reshape(1, d), w)


def _mem_kv_kernel(mem_ref, g_ref, w_ref, kg_ref, k_ref, v_ref):
    memn = _rms(mem_ref[...], g_ref[...]).astype(BF16)
    kv = _dot(memn, w_ref[...])
    kg = kg_ref[...]
    for h in range(XA_HEADS):
        sl = slice(h * XA_HEAD_DIM, (h + 1) * XA_HEAD_DIM)
        k_ref[:, sl] = _rms(kv[:, sl], kg).astype(BF16)
    v_ref[...] = kv[:, XA_WIDTH:].astype(BF16)


def _mem_kv(mem, mem_norm_g, w_kv, k_g):
    b, nm, d = mem.shape
    depth = w_kv.shape[0]
    kv_shape = jax.ShapeDtypeStruct((depth, b, nm, XA_WIDTH), BF16)
    kv_spec = pl.BlockSpec((None, None, nm, XA_WIDTH), lambda l, i: (l, i, 0, 0))
    return pl.pallas_call(
        _mem_kv_kernel,
        grid=(depth, b),
        in_specs=[pl.BlockSpec((None, nm, d), lambda l, i: (i, 0, 0)),
                  pl.BlockSpec((None, 1, d), lambda l, i: (l, 0, 0)),
                  pl.BlockSpec((None, d, 2 * XA_WIDTH), lambda l, i: (l, 0, 0)),
                  pl.BlockSpec((None, 1, XA_HEAD_DIM), lambda l, i: (l, 0, 0))],
        out_specs=[kv_spec, kv_spec],
        out_shape=[kv_shape, kv_shape],
        compiler_params=_cparams("parallel", "parallel"),
        name="mem_kv",
    )(mem, mem_norm_g.reshape(depth, 1, d), w_kv, k_g.reshape(depth, 1, XA_HEAD_DIM))


def _out_proj_kernel(x_ref, mix_ref, qx_ref, k_ref, v_ref, qg_ref, w_ref, ng_ref, o_ref, h_ref, *,
                     head_major):
    heads = range(XA_HEADS)
    sl = [slice(h * XA_HEAD_DIM, (h + 1) * XA_HEAD_DIM) for h in heads]
    qg = qg_ref[...] * (XA_HEAD_DIM ** -0.5)
    for r0 in range(0, x_ref.shape[0], OUT_PROJ_SUB):
        rows = slice(r0, r0 + OUT_PROJ_SUB)
        qn = [_rms(qx_ref[rows, sl[h]].astype(F32), qg).astype(BF16) for h in heads]
        s = [_dot_nt(qn[h], k_ref[:, sl[h]]) for h in heads]
        e = [jnp.exp(s[h] - jnp.max(s[h], axis=-1, keepdims=True)) for h in heads]
        xa = [(_dot(e[h].astype(BF16), v_ref[:, sl[h]])
               / jnp.sum(e[h], axis=-1, keepdims=True)).astype(BF16) for h in heads]
        xa = jnp.concatenate(xa, axis=-1)
        if head_major:
            mix = jnp.concatenate([mix_ref[h, rows, :] for h in range(mix_ref.shape[0])], axis=-1)
        else:
            mix = mix_ref[rows, :]
        d_mix = mix.shape[-1]
        o = x_ref[rows, :] + _dot(mix, w_ref[:d_mix, :]) + _dot(xa, w_ref[d_mix:, :])
        o_ref[rows, :] = o
        h_ref[rows, :] = _rms(o, ng_ref[...]).astype(BF16)


def _out_proj(x, mix, qx, kx, vx, q_g, w_out, next_g, layer, *, batch, head_major, tm=1024):
    m, d = x.shape
    t = m // batch
    nt = t // tm
    nm = kx.shape[2]
    if head_major:
        nh = mix.shape[0]
        mix_spec = pl.BlockSpec((nh, tm, LANES), lambda b, i: (0, b * nt + i, 0))
    else:
        mix_spec = pl.BlockSpec((tm, mix.shape[1]), lambda b, i: (b * nt + i, 0))
    kv_spec = pl.BlockSpec((None, None, nm, XA_WIDTH), lambda b, i: (layer, b, 0, 0))
    row_spec = lambda w: pl.BlockSpec((tm, w), lambda b, i: (b * nt + i, 0))
    return pl.pallas_call(
        functools.partial(_out_proj_kernel, head_major=head_major),
        grid=(batch, nt),
        in_specs=[row_spec(d), mix_spec, row_spec(XA_WIDTH), kv_spec, kv_spec,
                  pl.BlockSpec((1, XA_HEAD_DIM), lambda b, i: (0, 0)),
                  pl.BlockSpec(w_out.shape, lambda b, i: (0, 0)),
                  pl.BlockSpec((1, d), lambda b, i: (0, 0))],
        out_specs=[row_spec(d), row_spec(d)],
        out_shape=[jax.ShapeDtypeStruct((m, d), F32), jax.ShapeDtypeStruct((m, d), BF16)],
        compiler_params=_cparams("parallel", "parallel"),
        name="out_proj",
    )(x, mix, qx, kx, vx, q_g.reshape(1, XA_HEAD_DIM), w_out, next_g.reshape(1, d))


def _mlp_kernel(x_ref, h_ref, wu_ref, wd_ref, *rest, emit_next):
    if emit_next:
        ng_ref, o_ref, hn_ref, acc_ref = rest
    else:
        o_ref, acc_ref = rest
    k = pl.program_id(1)

    @pl.when(k == 0)
    def _():
        acc_ref[...] = x_ref[...]

    u = jnp.maximum(_dot(h_ref[...], wu_ref[...]), 0.0)
    acc_ref[...] += _dot((u * u).astype(BF16), wd_ref[...])

    @pl.when(k == pl.num_programs(1) - 1)
    def _():
        o = acc_ref[...]
        o_ref[...] = o
        if emit_next:
            hn_ref[...] = _rms(o, ng_ref[...]).astype(BF16)


def _mlp(x, h, w_up, w_down, next_g=None, *, tm=1024, th=1024):
    m, d = x.shape
    hid = w_up.shape[1]
    assert m % tm == 0 and hid % th == 0
    emit_next = next_g is not None
    row = lambda: pl.BlockSpec((tm, d), lambda i, k: (i, 0))
    args = [x, h, w_up, w_down]
    in_specs = [row(), row(),
                pl.BlockSpec((d, th), lambda i, k: (0, k)),
                pl.BlockSpec((th, d), lambda i, k: (k, 0))]
    out_specs = [row()]
    out_shape = [jax.ShapeDtypeStruct((m, d), F32)]
    if emit_next:
        args.append(next_g.reshape(1, d))
        in_specs.append(pl.BlockSpec((1, d), lambda i, k: (0, 0)))
        out_specs.append(row())
        out_shape.append(jax.ShapeDtypeStruct((m, d), BF16))
    return pl.pallas_call(
        functools.partial(_mlp_kernel, emit_next=emit_next),
        grid=(m // tm, hid // th),
        in_specs=in_specs,
        out_specs=out_specs,
        out_shape=out_shape,
        scratch_shapes=[pltpu.VMEM((tm, d), F32)],
        compiler_params=_cparams("parallel", "arbitrary"),
        name="mlp",
    )(*args)


def _hgrn_levels(tile):
    levels = []
    s = 1
    while s < tile:
        levels.append(s)
        s *= 2
    return tuple(levels)


def _hgrn_pair_masks(tile):
    levels = _hgrn_levels(tile)
    i = np.arange(tile)[:, None]
    j = np.arange(tile)[None, :]
    return np.stack([((i // s == j // s + 1) & ((j // s) % 2 == 0)) for s in levels]).astype(np.float32)


def _hgrn_exponent_matrix(tile):
    levels = _hgrn_levels(tile)
    w = np.zeros((len(levels) * tile, tile), np.float32)
    for l, s in enumerate(levels[1:]):
        for i in range(tile):
            mid = (i // (2 * s)) * 2 * s + s - 1
            if (i // s) % 2 == 1:
                w[l * tile + i, mid + 1:i + 1] = 1.0
            else:
                w[l * tile + i, i + 1:mid + 1] = 1.0
    for i in range(tile):
        w[(len(levels) - 1) * tile + i, :i + 1] = 1.0
    return np.concatenate([w, w], axis=1)


def _hgrn_kernel(q_ref, f_ref, i_ref, g_ref, lg_ref, og_ref, w_ref, pm_ref, o_ref, st_ref, *,
                 layer, tile):
    levels = _hgrn_levels(tile)
    n_lev = len(levels)

    @pl.when(pl.program_id(1) == 0)
    def _():
        st_ref[...] = jnp.zeros_like(st_ref)

    row = lax.broadcasted_iota(jnp.int32, (tile, LANES), 0)

    heads = range(HGRN_HEADS)

    def gates(h):
        lg = lg_ref[h]
        e = jnp.exp(lg - jnp.max(lg, axis=0, keepdims=True))
        lb_num = jnp.zeros((1, LANES), F32)
        for r in range(1, layer + 1):
            lb_num = lb_num + e[r:r + 1, :]
        lb = lb_num / jnp.sum(e, axis=0, keepdims=True)
        f = lb + (1.0 - lb) * _sigmoid(f_ref[h].astype(F32))
        fc = jnp.maximum(f, MIN_FORGET)
        logf = jnp.log(fc)
        hi = logf.astype(BF16)
        lo = (logf - hi.astype(F32)).astype(BF16)
        ex = _dot(w_ref[...], jnp.concatenate([hi, lo], axis=0))
        return 1.0 - f, fc, ex

    def scaled(h, k, fc, ex):
        q = q_ref[h].astype(F32)
        xs = [jnp.where((row & 1) != 0, q * jnp.minimum(fc, 1.0), k).astype(BF16)]
        for l, s in enumerate(levels[1:]):
            el = jnp.exp(jnp.minimum(ex[l * tile:(l + 1) * tile, :], 0.0))
            xs.append((jnp.where((row & s) != 0, q, k) * el).astype(BF16))
        b = ex[(n_lev - 1) * tile:, :]
        b_last = b[tile - 1:tile, :]
        qh = (q * jnp.exp(b)).astype(BF16)
        kh = (k * jnp.exp(b_last - b)).astype(BF16)
        diag = jnp.sum(q * k, axis=-1, keepdims=True)
        return xs, qh, kh, jnp.exp(b_last), diag

    def pair_weights(xs):
        attn = _dot_nt(xs[0], xs[0]) * pm_ref[0]
        for l in range(1, n_lev):
            attn = attn + _dot_nt(xs[l], xs[l]) * pm_ref[l]
        return attn.astype(BF16)

    def output(h, attn, qh, kh, dec, diag):
        v = i_ref[h]
        st = st_ref[h]
        o = diag * v.astype(F32) + _dot(attn, v) + _dot_nt(qh, st.astype(BF16))
        st_ref[h] = dec * st + _dot_tn(v, kh)
        gate = g_ref[h].astype(F32)
        o_ref[h] = (_rms(o, og_ref[h]) * (gate * _sigmoid(gate))).astype(BF16)

    s1 = [gates(h) for h in heads]
    s2 = [scaled(h, *s1[h]) for h in heads]
    s3 = [pair_weights(s2[h][0]) for h in heads]
    for h in heads:
        output(h, s3[h], *s2[h][1:])


def _hgrn(proj_hm, lb_logits, o_norm_g, layer, *, batch):
    _, m, _ = proj_hm.shape
    tile = HGRN_TILE
    nt = (m // batch) // tile
    depth = lb_logits.shape[0]
    wmat = jnp.asarray(_hgrn_exponent_matrix(tile), BF16)
    pmask = jnp.asarray(_hgrn_pair_masks(tile), F32)
    lg =lb_logits.reshape(depth, HGRN_HEADS, HGRN_HEAD_DIM).transpose(1, 0, 2)
    sec = lambda s: pl.BlockSpec((HGRN_HEADS, tile, LANES), lambda b, t: (s, b * nt + t, 0))
    return pl.pallas_call(
        functools.partial(_hgrn_kernel, layer=layer, tile=tile),
        grid=(batch, nt),
        in_specs=[sec(0), sec(1), sec(2), sec(3),
                  pl.BlockSpec((HGRN_HEADS, depth, LANES), lambda b, t: (0, 0, 0)),
                  pl.BlockSpec((HGRN_HEADS, 1, LANES), lambda b, t: (0, 0, 0)),
                  pl.BlockSpec(wmat.shape, lambda b, t: (0, 0)),
                  pl.BlockSpec(pmask.shape, lambda b, t: (0, 0, 0))],
        out_specs=pl.BlockSpec((HGRN_HEADS, tile, LANES), lambda b, t: (0, b * nt + t, 0)),
        out_shape=jax.ShapeDtypeStruct((HGRN_HEADS, m, LANES), BF16),
        scratch_shapes=[pltpu.VMEM((HGRN_HEADS, HGRN_HEAD_DIM, HGRN_HEAD_DIM), F32)],
        compiler_params=_cparams("parallel", "arbitrary"),
        name="hgrn2",
    )(proj_hm, proj_hm, proj_hm, proj_hm, lg,
      o_norm_g.reshape(HGRN_HEADS, 1, HGRN_HEAD_DIM), wmat, pmask)


def _rope_table_kernel(pos_ref, cos_ref, sin_ref):
    lane = lax.broadcasted_iota(jnp.int32, (1, LANES), 1)
    half = SWA_HEAD_DIM // 2
    idx = (lane % half).astype(F32)
    inv_freq = jnp.exp(idx * (-np.log(ROPE_THETA) / half))
    ang = pos_ref[...].astype(F32) * inv_freq
    first = (lane % SWA_HEAD_DIM) < half
    cos_ref[...] = jnp.cos(ang)
    sin_ref[...] = jnp.where(first, -1.0, 1.0) * jnp.sin(ang)


def _rope_tables(positions, tm=1024):
    m = positions.size
    shape = jax.ShapeDtypeStruct((m, LANES), F32)
    spec = pl.BlockSpec((tm, LANES), lambda i: (i, 0))
    return pl.pallas_call(
        _rope_table_kernel,
        grid=(m // tm,),
        in_specs=[pl.BlockSpec((tm, 1), lambda i: (i, 0))],
        out_specs=[spec, spec],
        out_shape=[shape, shape],
        compiler_params=_cparams("parallel"),
        name="rope_tables",
    )(positions.reshape(m, 1))


def _swa_partner(lane):
    half = SWA_HEAD_DIM // 2
    return np.where(lane % SWA_HEAD_DIM < half, lane + half, lane - half)


def _swa_reduce_permute_matrix():
    mat = np.zeros((2 * LANES, 2 * LANES), np.float32)
    lane = np.arange(LANES)
    mat[:LANES, :LANES] = (lane[:, None] // SWA_HEAD_DIM == lane[None, :] // SWA_HEAD_DIM) / SWA_HEAD_DIM
    mat[LANES + _swa_partner(lane), LANES + lane] = 1.0
    return mat


def _swa_norm_rope(xb, mat, cos_g, sin_g):
    x = xb.astype(F32)
    r = _dot(jnp.concatenate([(x * x).astype(BF16), xb], axis=1), mat)
    return lax.rsqrt(r[:, :LANES] + RMS_EPS) * (x * cos_g + r[:, LANES:] * sin_g)


def _swa_kernel(sink_ref, q_ref, kc_ref, kp_ref, vc_ref, vp_ref, cc_ref, sc_ref, cp_ref, sp_ref,
                qg_ref, qgp_ref, kg_ref, kgp_ref, mat_ref, o_ref):
    blk = SWA_WINDOW
    n = pl.program_id(1)
    lane = lax.broadcasted_iota(jnp.int32, (1, LANES), 1)
    lo = lane < SWA_HEAD_DIM
    pairs = range(SWA_Q_HEADS // 2)
    group = SWA_Q_HEADS // SWA_KV_HEADS
    mat = mat_ref[...]

    cos_c, sin_c = cc_ref[...], sc_ref[...]
    kk = jnp.concatenate(
        [_swa_norm_rope(kp_ref[...], mat, cp_ref[...] * kg_ref[...], sp_ref[...] * kgp_ref[...]),
         _swa_norm_rope(kc_ref[...], mat, cos_c * kg_ref[...], sin_c * kgp_ref[...])], axis=0)
    vv = jnp.concatenate([vp_ref[...], vc_ref[...]], axis=0).astype(F32)
    kk_sw = pltpu.roll(kk, SWA_HEAD_DIM, axis=1)
    vv_sw = pltpu.roll(vv, SWA_HEAD_DIM, axis=1)
    ones = jnp.ones((2 * blk, LANES), BF16)
    k_cat = [jnp.concatenate([jnp.where(lo, a, 0.0), jnp.where(lo, 0.0, b)], axis=0).astype(BF16)
             for a, b in ((kk, kk_sw), (kk_sw, kk))]
    v_lo = [jnp.concatenate([jnp.where(lo, a, 0.0).astype(BF16), ones], axis=1) for a in (vv, vv_sw)]
    v_hi = [jnp.concatenate([jnp.where(lo, 0.0, a).astype(BF16), ones], axis=1) for a in (vv_sw, vv)]

    qi = lax.broadcasted_iota(jnp.int32, (blk, blk), 0)
    ki = lax.broadcasted_iota(jnp.int32, (blk, blk), 1)
    use_cur = ki <= qi
    prev_bias = jnp.where(n > 0, 0.0, MASK_VALUE)

    def weights(s_prev, s_cur, sink):
        s = jnp.where(use_cur, s_cur, s_prev + prev_bias)
        mx = jnp.maximum(jnp.max(s, axis=-1, keepdims=True), sink)
        e = jnp.exp(s - mx)
        e2 = jnp.concatenate([jnp.where(use_cur, 0.0, e), jnp.where(use_cur, e, 0.0)], axis=1)
        return e2.astype(BF16), jnp.exp(sink - mx)

    def attend(w, v_aug):
        r = _dot(w[0], v_aug)
        return r[:, :LANES] / (r[:, LANES:] + w[1])

    scale = SWA_HEAD_DIM ** -0.5
    cos_q, sin_q = cos_c * (qg_ref[...] * scale), sin_c * (qgp_ref[...] * scale)
    qr = [_swa_norm_rope(q_ref[:, p * LANES:(p + 1) * LANES], mat, cos_q, sin_q).astype(BF16)
          for p in pairs]
    sc = [_dot_nt(qr[p], k_cat[(2 * p) // group]) for p in pairs]
    wa = [weights(sc[p][:, 0:blk], sc[p][:, blk:2 * blk], sink_ref[2 * p]) for p in pairs]
    wb = [weights(sc[p][:, 2 * blk:3 * blk], sc[p][:, 3 * blk:], sink_ref[2 * p + 1])
          for p in pairs]
    for p in pairs:
        g = (2 * p) // group
        o_ref[:, p * LANES:(p + 1) * LANES] = (
            attend(wa[p], v_lo[g]) + attend(wb[p], v_hi[g])).astype(BF16)


def _swa(proj, cos, sin, q_g, k_g, sinks, *, batch):
    m = proj.shape[0]
    blk = SWA_WINDOW
    nb = (m // batch) // blk
    qw = SWA_Q_HEADS * SWA_HEAD_DIM
    kcol = qw // LANES
    cur = lambda b, n: b * nb + n
    prev = lambda b, n: b * nb + jnp.maximum(n - 1, 0)
    tile2 = lambda g: jnp.concatenate([g, g]).reshape(1, LANES)
    partner = _swa_partner(np.arange(LANES))
    mat = jnp.asarray(_swa_reduce_permute_matrix(), BF16)
    blk_spec = lambda row, col: pl.BlockSpec((blk, LANES), lambda b, n: (row(b, n), col))
    vec = lambda: pl.BlockSpec((1, LANES), lambda b, n: (0, 0))
    return pl.pallas_call(
        _swa_kernel,
        grid=(batch, nb),
        in_specs=[pl.BlockSpec(memory_space=pltpu.SMEM),
                  pl.BlockSpec((blk, qw), lambda b, n: (cur(b, n), 0)),
                  blk_spec(cur, kcol), blk_spec(prev, kcol),
                  blk_spec(cur, kcol + 1), blk_spec(prev, kcol + 1),
                  blk_spec(cur, 0), blk_spec(cur, 0), blk_spec(prev, 0), blk_spec(prev, 0),
                  vec(), vec(), vec(), vec(),
                  pl.BlockSpec(mat.shape, lambda b, n: (0, 0))],
        out_specs=pl.BlockSpec((blk, qw), lambda b, n: (cur(b, n), 0)),
        out_shape=jax.ShapeDtypeStruct((m, qw), BF16),
        compiler_params=_cparams("parallel", "parallel"),
        name="swa",
    )(sinks, proj, proj, proj, proj, proj, cos, sin, cos, sin,
      tile2(q_g), tile2(q_g)[:, partner], tile2(k_g), tile2(k_g)[:, partner], mat)


def _conv_kernel(a_ref, gate_ref, cw_ref, cb_ref, lg_ref, lb_ref, o_ref, u_ref, us_ref, y_ref, *,
                 tile):
    t = pl.program_id(1)
    halo = CONV_HALO

    @pl.when(t == 0)
    def _():
        u_ref[0:halo, :] = jnp.zeros((halo, u_ref.shape[1]), F32)

    @pl.when(t > 0)
    def _():
        u_ref[0:halo, :] = u_ref[tile:tile + halo, :]

    u_ref[halo:halo + tile, :] = a_ref[...].astype(F32) * _sigmoid(gate_ref[...].astype(F32))

    span = us_ref.shape[1]
    for r in range(1, SUBLANES):
        us_ref[r - 1] = u_ref[r:r + span, :]

    base = halo - (CONV_WIDTH - 1)
    width = u_ref.shape[1]
    for c0 in range(0, width, LANES):
        for r0 in range(0, tile, CONV_ROWS):
            acc = jnp.zeros((CONV_ROWS, LANES), F32)
            for w in range(CONV_WIDTH):
                r, a0 = (base + w) % SUBLANES, (base + w) // SUBLANES * SUBLANES + r0
                src = u_ref if r == 0 else us_ref.at[r - 1]
                acc = acc + src[a0:a0 + CONV_ROWS, c0:c0 + LANES] * cw_ref[w:w + 1, c0:c0 + LANES]
            y_ref[r0:r0 + CONV_ROWS, c0:c0 + LANES] = acc

    y = y_ref[...] + cb_ref[...]
    mu = jnp.mean(y, axis=-1, keepdims=True)
    yc = y - mu
    var = jnp.mean(yc * yc, axis=-1, keepdims=True)
    z = yc * lax.rsqrt(var + LN_EPS) * lg_ref[...] + lb_ref[...]
    o_ref[...] = (z * _sigmoid(z)).astype(BF16)


def _conv(proj, conv_w, conv_b, ln_g, ln_b, *, batch):
    m = proj.shape[0]
    c = conv_w.shape[1]
    tile = CONV_TILE
    nt = (m // batch) // tile
    vec = lambda: pl.BlockSpec((1, c), lambda b, t: (0, 0))
    return pl.pallas_call(
        functools.partial(_conv_kernel, tile=tile),
        grid=(batch, nt),
        in_specs=[pl.BlockSpec((tile, c), lambda b, t: (b * nt + t, 0)),
                  pl.BlockSpec((tile, c), lambda b, t: (b * nt + t, 1)),
                  pl.BlockSpec((CONV_WIDTH, c), lambda b, t: (0, 0)),
                  vec(), vec(), vec()],
        out_specs=pl.BlockSpec((tile, c), lambda b, t: (b * nt + t, 0)),
        out_shape=jax.ShapeDtypeStruct((m, c), BF16),
        scratch_shapes=[pltpu.VMEM((CONV_HALO + tile, c), F32),
                        pltpu.VMEM((SUBLANES - 1, CONV_HALO + tile - SUBLANES, c), F32),
                        pltpu.VMEM((tile, c), F32)],
        compiler_params=_cparams("parallel", "arbitrary"),
        name="conformer_conv",
    )(proj, proj, conv_w, conv_b.reshape(1, c), ln_g.reshape(1, c), ln_b.reshape(1, c))


def kernel(x, mem, positions, norm_mix_g, norm_mlp_g, mem_norm_g, xa_w_kv, xa_q_norm_g, xa_k_norm_g, mlp_w_up, mlp_w_down, hgrn_lb_logits, a_w_in, a_o_norm_g, a_w_out, b_w_in, b_q_norm_g, b_k_norm_g, b_sinks, b_w_out, c_w_in, c_conv_w, c_conv_b, c_ln_g, c_ln_b, c_w_out):
    batch, seq, d = x.shape
    depth = norm_mix_g.shape[0]
    kinds = tuple(i % 3 for i in range(depth))
    xs = x.reshape(batch * seq, d)

    kx, vx = _mem_kv(mem, mem_norm_g, xa_w_kv.astype(BF16), xa_k_norm_g)
    if 1 in kinds:
        cos, sin = _rope_tables(positions)

    h = xs
    for layer in range(depth):
        kind = kinds[layer]
        idx = kinds[:layer].count(kind)
        last = layer == depth - 1
        if kind == 0:
            proj, qx = _in_proj(h, norm_mix_g[layer], a_w_in[idx].astype(BF16), head_major=True)
            mix = _hgrn(proj, hgrn_lb_logits, a_o_norm_g[idx], layer, batch=batch)
            w_out = a_w_out[idx]
        elif kind == 1:
            proj, qx = _in_proj(h, norm_mix_g[layer], b_w_in[idx].astype(BF16), head_major=False)
            mix = _swa(proj, cos, sin, b_q_norm_g[idx], b_k_norm_g[idx], b_sinks[idx], batch=batch)
            w_out = b_w_out[idx]
        else:
            proj, qx = _in_proj(h, norm_mix_g[layer], c_w_in[idx].astype(BF16), head_major=False)
            mix = _conv(proj, c_conv_w[idx], c_conv_b[idx], c_ln_g[idx], c_ln_b[idx], batch=batch)
            w_out = c_w_out[idx]
        xs, h = _out_proj(xs, mix, qx, kx, vx, xa_q_norm_g[layer], w_out.astype(BF16),
                          norm_mlp_g[layer], layer, batch=batch, head_major=(kind == 0))
        res = _mlp(xs, h, mlp_w_up[layer].astype(BF16), mlp_w_down[layer].astype(BF16),
                   None if last else norm_mix_g[layer + 1])
        xs, h = (res[0], None) if last else res
    return xs.reshape(batch, seq, d)
```

```python
import functools

import numpy as np
import jax
import jax.numpy as jnp
from jax import lax
from jax.experimental import pallas as pl
from jax.experimental.pallas import tpu as pltpu

F32 = jnp.float32
BF16 = jnp.bfloat16

RMS_EPS = 1e-6
LN_EPS = 1e-5
MASK_VALUE = -1e30
MIN_FORGET = 1e-20
ROPE_THETA = 10000.0
LOG2_E = 1.4426950408889634

LANES = 128
SUBLANES = 8

N_MEM = 256
XA_HEADS = 4
XA_HEAD_DIM = 128
XA_WIDTH = XA_HEADS * XA_HEAD_DIM
HGRN_HEADS = 8
HGRN_HEAD_DIM = 128
HGRN_TILE = 128
HGRN_STEP_TILES = 2
OUT_PROJ_SUB = 256
SWA_HEAD_DIM = 64
SWA_Q_HEADS = 16
SWA_KV_HEADS = 2
SWA_WINDOW = 128
CONV_WIDTH = 31
CONV_HALO = 32
CONV_TILE = 128
CONV_ROWS = 64

VMEM_LIMIT = 56 * 1024 * 1024


def _cparams(*sem):
    return pltpu.CompilerParams(dimension_semantics=sem, vmem_limit_bytes=VMEM_LIMIT)


def _dot(a, b):
    return jnp.dot(a, b, preferred_element_type=F32)


def _dot_nt(a, b):
    return lax.dot_general(a, b, (((1,), (1,)), ((), ())), preferred_element_type=F32)


def _dot_tn(a, b):
    return lax.dot_general(a, b, (((0,), (0,)), ((), ())), preferred_element_type=F32)


def _rms(x, g):
    return x * lax.rsqrt(jnp.mean(x * x, axis=-1, keepdims=True) + RMS_EPS) * g


def _sigmoid(x):
    return 1.0 / (1.0 + jnp.exp(-x))


def _cast_weight_once(w_ref, wb_ref, first_step, tn):
    @pl.when(first_step)
    def _():
        for c0 in range(0, w_ref.shape[1], tn):
            wb_ref[:, c0:c0 + tn] = w_ref[:, c0:c0 + tn].astype(BF16)


def _in_proj_kernel(x_ref, g_ref, w_ref, mix_ref, qx_ref, wb_ref, *, n_mix, tn, head_major):
    _cast_weight_once(w_ref, wb_ref, pl.program_id(0) == 0, tn)
    h = x_ref[...] if x_ref.dtype == BF16 else _rms(x_ref[...], g_ref[...]).astype(BF16)
    for c0 in range(0, n_mix, tn):
        r = _dot(h, wb_ref[:, c0:c0 + tn]).astype(BF16)
        if head_major:
            for j in range(tn // LANES):
                mix_ref[c0 // LANES + j] = r[:, j * LANES:(j + 1) * LANES]
        else:
            mix_ref[:, c0:c0 + tn] = r
    qx_ref[...] = _dot(h, wb_ref[:, n_mix:]).astype(BF16)


def _in_proj(x, g, w_all, idx, *, head_major, tm=512):
    m, d = x.shape
    n = w_all.shape[2]
    n_mix = n - XA_WIDTH
    tn = 512 if n_mix % 512 == 0 else 256
    assert m % tm == 0 and n_mix % tn == 0
    if head_major:
        mix_shape = jax.ShapeDtypeStruct((n_mix // LANES, m, LANES), BF16)
        mix_spec = pl.BlockSpec((n_mix // LANES, tm, LANES), lambda i: (0, i, 0))
    else:
        mix_shape = jax.ShapeDtypeStruct((m, n_mix), BF16)
        mix_spec = pl.BlockSpec((tm, n_mix), lambda i: (i, 0))
    return pl.pallas_call(
        functools.partial(_in_proj_kernel, n_mix=n_mix, tn=tn, head_major=head_major),
        grid=(m // tm,),
        in_specs=[pl.BlockSpec((tm, d), lambda i: (i, 0)),
                  pl.BlockSpec((1, d), lambda i: (0, 0)),
                  pl.BlockSpec((None, d, n), lambda i: (idx, 0, 0), pipeline_mode=pl.Buffered(1))],
        out_specs=[mix_spec, pl.BlockSpec((tm, XA_WIDTH), lambda i: (i, 0))],
        out_shape=[mix_shape, jax.ShapeDtypeStruct((m, XA_WIDTH), BF16)],
        scratch_shapes=[pltpu.VMEM((d, n), BF16)],
        compiler_params=_cparams("arbitrary"),
        name="in_proj",
    )(x, g.reshape(1, d), w_all)


def _mem_kv_kernel(mem_ref, g_ref, w_ref, kg_ref, k_ref, v_ref):
    memn = _rms(mem_ref[...], g_ref[...]).astype(BF16)
    kv = _dot(memn, w_ref[...])
    kg = kg_ref[...]
    for h in range(XA_HEADS):
        sl = slice(h * XA_HEAD_DIM, (h + 1) * XA_HEAD_DIM)
        k_ref[:, sl] = _rms(kv[:, sl], kg).astype(BF16)
    v_ref[...] = kv[:, XA_WIDTH:].astype(BF16)


def _mem_kv(mem, mem_norm_g, w_kv, k_g):
    b, nm, d = mem.shape
    depth = w_kv.shape[0]
    rows = b * nm
    kv_shape = jax.ShapeDtypeStruct((depth, rows, XA_WIDTH), BF16)
    kv_spec = pl.BlockSpec((None, rows, XA_WIDTH), lambda l: (l, 0, 0))
    kx, vx = pl.pallas_call(
        _mem_kv_kernel,
        grid=(depth,),
        in_specs=[pl.BlockSpec((rows, d), lambda l: (0, 0)),
                  pl.BlockSpec((None, 1, d), lambda l: (l, 0, 0)),
                  pl.BlockSpec((None, d, 2 * XA_WIDTH), lambda l: (l, 0, 0)),
                  pl.BlockSpec((None, 1, XA_HEAD_DIM), lambda l: (l, 0, 0))],
        out_specs=[kv_spec, kv_spec],
        out_shape=[kv_shape, kv_shape],
        compiler_params=_cparams("parallel"),
        name="mem_kv",
    )(mem.reshape(rows, d), mem_norm_g.reshape(depth, 1, d), w_kv, k_g.reshape(depth, 1, XA_HEAD_DIM))
    return kx.reshape(depth, b, nm, XA_WIDTH), vx.reshape(depth, b, nm, XA_WIDTH)


def _out_proj_kernel(x_ref, mix_ref, qx_ref, k_ref, v_ref, qg_ref, wf_ref, ng_ref, o_ref, h_ref,
                     w_ref, *, head_major):
    first = (pl.program_id(0) == 0) & (pl.program_id(1) == 0)
    _cast_weight_once(wf_ref, w_ref, first, wf_ref.shape[1])
    heads = range(XA_HEADS)
    sl = [slice(h * XA_HEAD_DIM, (h + 1) * XA_HEAD_DIM) for h in heads]
    qg = qg_ref[...] * (XA_HEAD_DIM ** -0.5)
    for r0 in range(0, x_ref.shape[0], OUT_PROJ_SUB):
        rows = slice(r0, r0 + OUT_PROJ_SUB)
        qn = [_rms(qx_ref[rows, sl[h]].astype(F32), qg).astype(BF16) for h in heads]
        s = [_dot_nt(qn[h], k_ref[:, sl[h]]) for h in heads]
        e = [jnp.exp(s[h] - jnp.max(s[h], axis=-1, keepdims=True)) for h in heads]
        xa = [(_dot(e[h].astype(BF16), v_ref[:, sl[h]])
               / jnp.sum(e[h], axis=-1, keepdims=True)).astype(BF16) for h in heads]
        xa = jnp.concatenate(xa, axis=-1)
        if head_major:
            mix = jnp.concatenate([mix_ref[h, rows, :] for h in range(mix_ref.shape[0])], axis=-1)
        else:
            mix = mix_ref[rows, :]
        d_mix = mix.shape[-1]
        o = x_ref[rows, :] + _dot(mix, w_ref[:d_mix, :]) + _dot(xa, w_ref[d_mix:, :])
        o_ref[rows, :] = o
        h_ref[rows, :] = _rms(o, ng_ref[...]).astype(BF16)


def _out_proj(x, mix, qx, kx, vx, q_g, w_all, idx, next_g, layer, *, batch, head_major, tm=1024):
    m, d = x.shape
    d_in = w_all.shape[1]
    t = m // batch
    nt = t // tm
    nm = kx.shape[2]
    if head_major:
        nh = mix.shape[0]
        mix_spec = pl.BlockSpec((nh, tm, LANES), lambda b, i: (0, b * nt + i, 0))
    else:
        mix_spec = pl.BlockSpec((tm, mix.shape[1]), lambda b, i: (b * nt + i, 0))
    kv_spec = pl.BlockSpec((None, None, nm, XA_WIDTH), lambda b, i: (layer, b, 0, 0))
    row_spec = lambda w: pl.BlockSpec((tm, w), lambda b, i: (b * nt + i, 0))
    return pl.pallas_call(
        functools.partial(_out_proj_kernel, head_major=head_major),
        grid=(batch, nt),
        in_specs=[row_spec(d), mix_spec, row_spec(XA_WIDTH), kv_spec, kv_spec,
                  pl.BlockSpec((1, XA_HEAD_DIM), lambda b, i: (0, 0)),
                  pl.BlockSpec((None, d_in, d), lambda b, i: (idx, 0, 0),
                               pipeline_mode=pl.Buffered(1)),
                  pl.BlockSpec((1, d), lambda b, i: (0, 0))],
        out_specs=[row_spec(d), row_spec(d)],
        out_shape=[jax.ShapeDtypeStruct((m, d), F32), jax.ShapeDtypeStruct((m, d), BF16)],
        scratch_shapes=[pltpu.VMEM((d_in, d), BF16)],
        compiler_params=_cparams("arbitrary", "arbitrary"),
        name="out_proj",
    )(x, mix, qx, kx, vx, q_g.reshape(1, XA_HEAD_DIM), w_all, next_g.reshape(1, d))


def _mlp_kernel(x_ref, h_ref, wu_ref, wd_ref, *rest, emit_next, th, sub):
    if emit_next:
        ng_ref, o_ref, hn_ref = rest
    else:
        (o_ref,) = rest
    for r0 in range(0, x_ref.shape[0], sub):
        rows = slice(r0, r0 + sub)
        h = h_ref[rows, :]
        o = x_ref[rows, :]
        for c0 in range(0, wu_ref.shape[1], th):
            u = jnp.maximum(_dot(h, wu_ref[:, c0:c0 + th]), 0.0)
            o = o + _dot((u * u).astype(BF16), wd_ref[c0:c0 + th, :])
        o_ref[rows, :] = o
        if emit_next:
            hn_ref[rows, :] = _rms(o, ng_ref[...]).astype(BF16)


def _mlp(x, h, w_up, w_down, next_g=None, *, tm=512, th=1024, sub=256):
    m, d = x.shape
    hid = w_up.shape[1]
    assert m % tm == 0 and hid % th == 0 and tm % sub == 0
    emit_next = next_g is not None
    row = lambda: pl.BlockSpec((tm, d), lambda i: (i, 0))
    resident = lambda shape: pl.BlockSpec(shape, lambda i: (0, 0), pipeline_mode=pl.Buffered(1))
    args = [x, h, w_up, w_down]
    in_specs = [row(), row(), resident(w_up.shape), resident(w_down.shape)]
    out_specs = [row()]
    out_shape = [jax.ShapeDtypeStruct((m, d), F32)]
    if emit_next:
        args.append(next_g.reshape(1, d))
        in_specs.append(pl.BlockSpec((1, d), lambda i: (0, 0)))
        out_specs.append(row())
        out_shape.append(jax.ShapeDtypeStruct((m, d), BF16))
    return pl.pallas_call(
        functools.partial(_mlp_kernel, emit_next=emit_next, th=th, sub=sub),
        grid=(m // tm,),
        in_specs=in_specs,
        out_specs=out_specs,
        out_shape=out_shape,
        compiler_params=_cparams("parallel"),
        name="mlp",
    )(*args)


def _hgrn_levels(tile):
    levels = []
    s = 1
    while s < tile:
        levels.append(s)
        s *= 2
    return tuple(levels)


def _hgrn_pair_masks(tile):
    levels = _hgrn_levels(tile)
    i = np.arange(tile)[:, None]
    j = np.arange(tile)[None, :]
    return np.stack([((i // s == j // s + 1) & ((j // s) % 2 == 0)) for s in levels]).astype(np.float32)


def _hgrn_exponent_matrix(tile):
    levels = _hgrn_levels(tile)
    w = np.zeros((len(levels) * tile, tile), np.float32)
    for l, s in enumerate(levels[1:]):
        for i in range(tile):
            mid = (i // (2 * s)) * 2 * s + s - 1
            if (i // s) % 2 == 1:
                w[l * tile + i, mid + 1:i + 1] = 1.0
            else:
                w[l * tile + i, i + 1:mid + 1] = 1.0
    for i in range(tile):
        w[(len(levels) - 1) * tile + i, :i + 1] = 1.0
    return np.concatenate([w, w], axis=1)


def _hgrn_kernel(q_ref, f_ref, i_ref, g_ref, lg_ref, og_ref, w_ref, pm_ref, o_ref, st_ref, *,
                 layer, tile):
    levels = _hgrn_levels(tile)
    n_lev = len(levels)

    @pl.when(pl.program_id(1) == 0)
    def _():
        st_ref[...] = jnp.zeros_like(st_ref)

    row = lax.broadcasted_iota(jnp.int32, (tile, LANES), 0)

    items = [(h, slice(r0, r0 + tile)) for r0 in range(0, q_ref.shape[1], tile)
             for h in range(HGRN_HEADS)]

    def gates(h, rows):
        lg = lg_ref[h]
        e = jnp.exp(lg - jnp.max(lg, axis=0, keepdims=True))
        lb_num = jnp.zeros((1, LANES), F32)
        for r in range(1, layer + 1):
            lb_num = lb_num + e[r:r + 1, :]
        lb = lb_num / jnp.sum(e, axis=0, keepdims=True)
        f = lb + (1.0 - lb) * _sigmoid(f_ref[h, rows, :].astype(F32))
        fc = jnp.maximum(f, MIN_FORGET)
        logf = jnp.log(fc) * LOG2_E
        hi = logf.astype(BF16)
        lo = (logf - hi.astype(F32)).astype(BF16)
        return 1.0 - f, fc, jnp.concatenate([hi, lo], axis=0)

    def exponents(split_a, split_b):
        ex = _dot(w_ref[...], jnp.concatenate([split_a, split_b], axis=1))
        return ex[:, :LANES], ex[:, LANES:]

    def scaled(h, rows, k, fc, ex):
        q = q_ref[h, rows, :].astype(F32)
        xs = [jnp.where((row & 1) != 0, q * jnp.minimum(fc, 1.0), k).astype(BF16)]
        for l, s in enumerate(levels[1:]):
            el = jnp.exp2(jnp.minimum(ex[l * tile:(l + 1) * tile, :], 0.0))
            xs.append((jnp.where((row & s) != 0, q, k) * el).astype(BF16))
        b = ex[(n_lev - 1) * tile:, :]
        b_last = b[tile - 1:tile, :]
        qh = (q * jnp.exp2(b)).astype(BF16)
        kh = (k * jnp.exp2(b_last - b)).astype(BF16)
        diag = jnp.sum(q * k, axis=-1, keepdims=True)
        return xs, qh, kh, jnp.exp2(b_last), diag

    def pair_weights(xs):
        attn = _dot_nt(xs[0], xs[0]) * pm_ref[0]
        for l in range(1, n_lev):
            attn = attn + _dot_nt(xs[l], xs[l]) * pm_ref[l]
        return attn.astype(BF16)

    def output(h, rows, attn, qh, kh, dec, diag):
        v = i_ref[h, rows, :]
        st = st_ref[h]
        o = diag * v.astype(F32) + _dot(attn, v) + _dot_nt(qh, st.astype(BF16))
        st_ref[h] = dec * st + _dot_tn(v, kh)
        gate = g_ref[h, rows, :].astype(F32)
        o_ref[h, rows, :] = (_rms(o, og_ref[h]) * (gate * _sigmoid(gate))).astype(BF16)

    n = range(len(items))
    s1 = [gates(*items[i]) for i in n]
    ex = [e for i in range(0, len(items), 2) for e in exponents(s1[i][2], s1[i + 1][2])]
    s2 = [scaled(*items[i], s1[i][0], s1[i][1], ex[i]) for i in n]
    s3 = [pair_weights(s2[i][0]) for i in n]
    for i in n:
        output(*items[i], s3[i], *s2[i][1:])


def _hgrn(proj_hm, lb_logits, o_norm_g, layer, *, batch):
    _, m, _ = proj_hm.shape
    tile = HGRN_TILE
    rows = HGRN_STEP_TILES * tile
    nt = (m // batch) // rows
    assert nt * rows * batch == m
    depth = lb_logits.shape[0]
    wmat = jnp.asarray(_hgrn_exponent_matrix(tile), BF16)
    pmask = jnp.asarray(_hgrn_pair_masks(tile), F32)
    lg = lb_logits.reshape(depth, HGRN_HEADS, HGRN_HEAD_DIM).transpose(1, 0, 2)
    sec = lambda s: pl.BlockSpec((HGRN_HEADS, rows, LANES), lambda b, t: (s, b * nt + t, 0))
    return pl.pallas_call(
        functools.partial(_hgrn_kernel, layer=layer, tile=tile),
        grid=(batch, nt),
        in_specs=[sec(0), sec(1), sec(2), sec(3),
                  pl.BlockSpec((HGRN_HEADS, depth, LANES), lambda b, t: (0, 0, 0)),
                  pl.BlockSpec((HGRN_HEADS, 1, LANES), lambda b, t: (0, 0, 0)),
                  pl.BlockSpec(wmat.shape, lambda b, t: (0, 0)),
                  pl.BlockSpec(pmask.shape, lambda b, t: (0, 0, 0))],
        out_specs=pl.BlockSpec((HGRN_HEADS, rows, LANES), lambda b, t: (0, b * nt + t, 0)),
        out_shape=jax.ShapeDtypeStruct((HGRN_HEADS, m, LANES), BF16),
        scratch_shapes=[pltpu.VMEM((HGRN_HEADS, HGRN_HEAD_DIM, HGRN_HEAD_DIM), F32)],
        compiler_params=_cparams("parallel", "arbitrary"),
        name="hgrn2",
    )(proj_hm, proj_hm, proj_hm, proj_hm, lg,
      o_norm_g.reshape(HGRN_HEADS, 1, HGRN_HEAD_DIM), wmat, pmask)


def _rope_table_kernel(pos_ref, cos_ref, sin_ref):
    lane = lax.broadcasted_iota(jnp.int32, (1, LANES), 1)
    half = SWA_HEAD_DIM // 2
    idx = (lane % half).astype(F32)
    inv_freq = jnp.exp(idx * (-np.log(ROPE_THETA) / half))
    ang = pos_ref[...].astype(F32) * inv_freq
    first = (lane % SWA_HEAD_DIM) < half
    cos_ref[...] = jnp.cos(ang)
    sin_ref[...] = jnp.where(first, -1.0, 1.0) * jnp.sin(ang)


def _rope_tables(positions, tm=1024):
    m = positions.size
    shape = jax.ShapeDtypeStruct((m, LANES), F32)
    spec = pl.BlockSpec((tm, LANES), lambda i: (i, 0))
    return pl.pallas_call(
        _rope_table_kernel,
        grid=(m // tm,),
        in_specs=[pl.BlockSpec((tm, 1), lambda i: (i, 0))],
        out_specs=[spec, spec],
        out_shape=[shape, shape],
        compiler_params=_cparams("parallel"),
        name="rope_tables",
    )(positions.reshape(m, 1))


def _swa_partner(lane):
    half = SWA_HEAD_DIM // 2
    return np.where(lane % SWA_HEAD_DIM < half, lane + half, lane - half)


def _swa_reduce_permute_matrix():
    mat = np.zeros((2 * LANES, 2 * LANES), np.float32)
    lane = np.arange(LANES)
    mat[:LANES, :LANES] = (lane[:, None] // SWA_HEAD_DIM == lane[None, :] // SWA_HEAD_DIM) / SWA_HEAD_DIM
    mat[LANES + _swa_partner(lane), LANES + lane] = 1.0
    return mat


def _swa_norm_rope(xb, mat, cos_g, sin_g):
    x = xb.astype(F32)
    r = _dot(jnp.concatenate([(x * x).astype(BF16), xb], axis=1), mat)
    return lax.rsqrt(r[:, :LANES] + RMS_EPS) * (x * cos_g + r[:, LANES:] * sin_g)


def _swa_kernel(sink_ref, q_ref, kc_ref, kp_ref, vc_ref, vp_ref, cc_ref, sc_ref, cp_ref, sp_ref,
                qg_ref, qgp_ref, kg_ref, kgp_ref, mat_ref, o_ref):
    blk = SWA_WINDOW
    n = pl.program_id(1)
    lane = lax.broadcasted_iota(jnp.int32, (1, LANES), 1)
    lo = lane < SWA_HEAD_DIM
    pairs = range(SWA_Q_HEADS // 2)
    group = SWA_Q_HEADS // SWA_KV_HEADS
    mat = mat_ref[...]

    cos_c, sin_c = cc_ref[...], sc_ref[...]
    kk = jnp.concatenate(
        [_swa_norm_rope(kp_ref[...], mat, cp_ref[...] * kg_ref[...], sp_ref[...] * kgp_ref[...]),
         _swa_norm_rope(kc_ref[...], mat, cos_c * kg_ref[...], sin_c * kgp_ref[...])], axis=0)
    vv = jnp.concatenate([vp_ref[...], vc_ref[...]], axis=0).astype(F32)
    kk_sw = pltpu.roll(kk, SWA_HEAD_DIM, axis=1)
    vv_sw = pltpu.roll(vv, SWA_HEAD_DIM, axis=1)
    ones = jnp.ones((2 * blk, LANES), BF16)
    k_cat = [jnp.concatenate([jnp.where(lo, a, 0.0), jnp.where(lo, 0.0, b)], axis=0).astype(BF16)
             for a, b in ((kk, kk_sw), (kk_sw, kk))]
    v_lo = [jnp.concatenate([jnp.where(lo, a, 0.0).astype(BF16), ones], axis=1) for a in (vv, vv_sw)]
    v_hi = [jnp.concatenate([jnp.where(lo, 0.0, a).astype(BF16), ones], axis=1) for a in (vv_sw, vv)]

    qi = lax.broadcasted_iota(jnp.int32, (blk, blk), 0)
    ki = lax.broadcasted_iota(jnp.int32, (blk, blk), 1)
    use_cur = ki <= qi
    prev_bias = jnp.where(n > 0, 0.0, MASK_VALUE)

    def weights(s_prev, s_cur, sink):
        s = jnp.where(use_cur, s_cur, s_prev + prev_bias)
        mx = jnp.maximum(jnp.max(s, axis=-1, keepdims=True), sink)
        e = jnp.exp(s - mx)
        e2 = jnp.concatenate([jnp.where(use_cur, 0.0, e), jnp.where(use_cur, e, 0.0)], axis=1)
        return e2.astype(BF16), jnp.exp(sink - mx)

    def attend(w, v_aug):
        r = _dot(w[0], v_aug)
        return r[:, :LANES] / (r[:, LANES:] + w[1])

    scale = SWA_HEAD_DIM ** -0.5
    cos_q, sin_q = cos_c * (qg_ref[...] * scale), sin_c * (qgp_ref[...] * scale)
    qr = [_swa_norm_rope(q_ref[:, p * LANES:(p + 1) * LANES], mat, cos_q, sin_q).astype(BF16)
          for p in pairs]
    sc = [_dot_nt(qr[p], k_cat[(2 * p) // group]) for p in pairs]
    wa = [weights(sc[p][:, 0:blk], sc[p][:, blk:2 * blk], sink_ref[2 * p]) for p in pairs]
    wb = [weights(sc[p][:, 2 * blk:3 * blk], sc[p][:, 3 * blk:], sink_ref[2 * p + 1])
          for p in pairs]
    for p in pairs:
        g = (2 * p) // group
        o_ref[:, p * LANES:(p + 1) * LANES] = (
            attend(wa[p], v_lo[g]) + attend(wb[p], v_hi[g])).astype(BF16)


def _swa(proj, cos, sin, q_g, k_g, sinks, *, batch):
    m = proj.shape[0]
    blk = SWA_WINDOW
    nb = (m // batch) // blk
    qw = SWA_Q_HEADS * SWA_HEAD_DIM
    kcol = qw // LANES
    cur = lambda b, n: b * nb + n
    prev = lambda b, n: b * nb + jnp.maximum(n - 1, 0)
    tile2 = lambda g: jnp.concatenate([g, g]).reshape(1, LANES)
    partner = _swa_partner(np.arange(LANES))
    mat = jnp.asarray(_swa_reduce_permute_matrix(), BF16)
    blk_spec = lambda row, col: pl.BlockSpec((blk, LANES), lambda b, n: (row(b, n), col))
    vec = lambda: pl.BlockSpec((1, LANES), lambda b, n: (0, 0))
    return pl.pallas_call(
        _swa_kernel,
        grid=(batch, nb),
        in_specs=[pl.BlockSpec(memory_space=pltpu.SMEM),
                  pl.BlockSpec((blk, qw), lambda b, n: (cur(b, n), 0)),
                  blk_spec(cur, kcol), blk_spec(prev, kcol),
                  blk_spec(cur, kcol + 1), blk_spec(prev, kcol + 1),
                  blk_spec(cur, 0), blk_spec(cur, 0), blk_spec(prev, 0), blk_spec(prev, 0),
                  vec(), vec(), vec(), vec(),
                  pl.BlockSpec(mat.shape, lambda b, n: (0, 0))],
        out_specs=pl.BlockSpec((blk, qw), lambda b, n: (cur(b, n), 0)),
        out_shape=jax.ShapeDtypeStruct((m, qw), BF16),
        compiler_params=_cparams("parallel", "parallel"),
        name="swa",
    )(sinks, proj, proj, proj, proj, proj, cos, sin, cos, sin,
      tile2(q_g), tile2(q_g)[:, partner], tile2(k_g), tile2(k_g)[:, partner], mat)


def _conv_kernel(a_ref, gate_ref, cw_ref, cb_ref, lg_ref, lb_ref, o_ref, u_ref, us_ref, y_ref, *,
                 tile):
    t = pl.program_id(1)
    halo = CONV_HALO

    @pl.when(t == 0)
    def _():
        u_ref[0:halo, :] = jnp.zeros((halo, u_ref.shape[1]), F32)

    @pl.when(t > 0)
    def _():
        u_ref[0:halo, :] = u_ref[tile:tile + halo, :]

    u_ref[halo:halo + tile, :] = a_ref[...].astype(F32) * _sigmoid(gate_ref[...].astype(F32))

    span = us_ref.shape[1]
    for r in range(1, SUBLANES):
        us_ref[r - 1] = u_ref[r:r + span, :]

    base = halo - (CONV_WIDTH - 1)
    width = u_ref.shape[1]
    for c0 in range(0, width, LANES):
        for r0 in range(0, tile, CONV_ROWS):
            acc = jnp.zeros((CONV_ROWS, LANES), F32)
            for w in range(CONV_WIDTH):
                r, a0 = (base + w) % SUBLANES, (base + w) // SUBLANES * SUBLANES + r0
                src = u_ref if r == 0 else us_ref.at[r - 1]
                acc = acc + src[a0:a0 + CONV_ROWS, c0:c0 + LANES] * cw_ref[w:w + 1, c0:c0 + LANES]
            y_ref[r0:r0 + CONV_ROWS, c0:c0 + LANES] = acc

    y = y_ref[...] + cb_ref[...]
    mu = jnp.mean(y, axis=-1, keepdims=True)
    yc = y - mu
    var = jnp.mean(yc * yc, axis=-1, keepdims=True)
    z = yc * lax.rsqrt(var + LN_EPS) * lg_ref[...] + lb_ref[...]
    o_ref[...] = (z * _sigmoid(z)).astype(BF16)


def _conv(proj, conv_w, conv_b, ln_g, ln_b, *, batch):
    m = proj.shape[0]
    c = conv_w.shape[1]
    tile = CONV_TILE
    nt = (m // batch) // tile
    vec = lambda: pl.BlockSpec((1, c), lambda b, t: (0, 0))
    return pl.pallas_call(
        functools.partial(_conv_kernel, tile=tile),
        grid=(batch, nt),
        in_specs=[pl.BlockSpec((tile, c), lambda b, t: (b * nt + t, 0)),
                  pl.BlockSpec((tile, c), lambda b, t: (b * nt + t, 1)),
                  pl.BlockSpec((CONV_WIDTH, c), lambda b, t: (0, 0)),
                  vec(), vec(), vec()],
        out_specs=pl.BlockSpec((tile, c), lambda b, t: (b * nt + t, 0)),
        out_shape=jax.ShapeDtypeStruct((m, c), BF16),
        scratch_shapes=[pltpu.VMEM((CONV_HALO + tile, c), F32),
                        pltpu.VMEM((SUBLANES - 1, CONV_HALO + tile - SUBLANES, c), F32),
                        pltpu.VMEM((tile, c), F32)],
        compiler_params=_cparams("parallel", "arbitrary"),
        name="conformer_conv",
    )(proj, proj, conv_w, conv_b.reshape(1, c), ln_g.reshape(1, c), ln_b.reshape(1, c))


def kernel(x, mem, positions, norm_mix_g, norm_mlp_g, mem_norm_g, xa_w_kv, xa_q_norm_g, xa_k_norm_g, mlp_w_up, mlp_w_down, hgrn_lb_logits, a_w_in, a_o_norm_g, a_w_out, b_w_in, b_q_norm_g, b_k_norm_g, b_sinks, b_w_out, c_w_in, c_conv_w, c_conv_b, c_ln_g, c_ln_b, c_w_out):
    batch, seq, d = x.shape
    depth = norm_mix_g.shape[0]
    kinds = tuple(i % 3 for i in range(depth))
    xs = x.reshape(batch * seq, d)

    kx, vx = _mem_kv(mem, mem_norm_g, xa_w_kv.astype(BF16), xa_k_norm_g)
    if 1 in kinds:
        cos, sin = _rope_tables(positions)

    h = xs
    for layer in range(depth):
        kind = kinds[layer]
        idx = kinds[:layer].count(kind)
        last = layer == depth - 1
        if kind == 0:
            proj, qx = _in_proj(h, norm_mix_g[layer], a_w_in, idx, head_major=True)
            mix = _hgrn(proj, hgrn_lb_logits, a_o_norm_g[idx], layer, batch=batch)
            w_out = a_w_out
        elif kind == 1:
            proj, qx = _in_proj(h, norm_mix_g[layer], b_w_in, idx, head_major=False)
            mix = _swa(proj, cos, sin, b_q_norm_g[idx], b_k_norm_g[idx], b_sinks[idx], batch=batch)
            w_out = b_w_out
        else:
            proj, qx = _in_proj(h, norm_mix_g[layer], c_w_in, idx, head_major=False)
            mix = _conv(proj, c_conv_w[idx], c_conv_b[idx], c_ln_g[idx], c_ln_b[idx], batch=batch)
            w_out = c_w_out
        xs, h = _out_proj(xs, mix, qx, kx, vx, xa_q_norm_g[layer], w_out, idx,
                          norm_mlp_g[layer], layer, batch=batch, head_major=(kind == 0))
        res = _mlp(xs, h, mlp_w_up[layer].astype(BF16), mlp_w_down[layer].astype(BF16),
                   None if last else norm_mix_g[layer + 1])
        xs, h = (res[0], None) if last else res
    return xs.reshape(batch, seq, d)
```
